```python
import functools
import jax, jax.numpy as jnp
from jax import lax
import numpy as np

D_MODEL = 1024
BATCH = 8
SEQ = 8192
DEPTH = 1
DEC_BATCH = 8
DEC_SEQ = 32
PAST_LEN = 4096

CHUNK = 64
H_A = 4
DK_A = 128
DV_A = 256
GATE_RANK = 16
GATE_TAU = 16.0
H_B = 8
HD_B = 64
BAND_CHUNKS = 8
REL_CLIP = 128
D_FF = 2816
CONV_W = 3
PLE_DIM = 256
EPS = 1e-6

QK_A = H_A * DK_A
V_A = H_A * DV_A
W_B = H_B * HD_B
REACH = BAND_CHUNKS * CHUNK
IN_SIZES = (QK_A, QK_A, V_A, V_A, GATE_RANK, W_B, W_B, W_B, D_MODEL, D_MODEL)
N_IN = sum(IN_SIZES)

kernel_name = 'hybrid_gla_chunkband_stream_step'


def rms_norm(x, g):
    xf = x.astype(jnp.float32)
    y = xf * lax.rsqrt(jnp.mean(xf * xf, axis=-1, keepdims=True) + EPS) * g.astype(jnp.float32)
    return y.astype(x.dtype)


def split_cols(z):
    idx = np.cumsum(IN_SIZES)[:-1].tolist()
    return jnp.split(z, idx, axis=-1)


def gla_branch(q, k, v, r, alr, w_a2, b_a2, g_gla, s0, block):
    B, T, _ = q.shape
    n = T // block
    f32 = jnp.float32
    log_a = jax.nn.log_sigmoid((alr @ w_a2 + b_a2).astype(f32)) / GATE_TAU

    def to_blocks(t, e):
        return t.astype(f32).reshape(B, n, block, H_A, e).transpose(1, 0, 3, 2, 4)

    qb = to_blocks(q, DK_A) * DK_A ** -0.5
    kb = to_blocks(k, DK_A)
    vb = to_blocks(v, DV_A)
    ab = to_blocks(log_a, DK_A)
    mask = jnp.tril(jnp.ones((block, block), dtype=bool))[:, :, None]

    def step(S, inp):
        qi, ki, vi, ai = inp
        b = jnp.cumsum(ai, axis=2)
        decay = jnp.exp(jnp.where(mask, b[:, :, :, None, :] - b[:, :, None, :, :], -jnp.inf))
        scores = jnp.einsum('bhic,bhjc,bhijc->bhij', qi, ki, decay)
        o = (jnp.einsum('bhij,bhjv->bhiv', scores, vi)
             + jnp.einsum('bhic,bhcv->bhiv', qi * jnp.exp(b), S))
        bl = b[:, :, -1]
        S = (jnp.exp(bl)[..., None] * S
             + jnp.einsum('bhjc,bhjv->bhcv', ki * jnp.exp(bl[:, :, None] - b), vi))
        return S, o

    s_fin, o = lax.scan(step, s0.astype(f32), (qb, kb, vb, ab))
    o = o.transpose(1, 0, 3, 2, 4).reshape(B, T, H_A, DV_A)
    o = o * lax.rsqrt(jnp.mean(o * o, axis=-1, keepdims=True) + EPS)
    o = o.reshape(B, T, V_A) * g_gla.astype(f32)
    o = o.astype(q.dtype) * jax.nn.silu(r)
    return o, s_fin.astype(s0.dtype)


def band_attend(q, k, v, q_pos, k_pos, rel_bias):
    s = jnp.einsum('bqhe,bkhe->bhqk', q, k).astype(jnp.float32) * HD_B ** -0.5
    rel = jnp.clip(q_pos[:, None] - k_pos[None, :], -REL_CLIP, REL_CLIP) + REL_CLIP
    bias = rel_bias.astype(jnp.float32)[:, rel]
    qc = q_pos // CHUNK
    kc = k_pos // CHUNK
    vis = ((k_pos[None, :] >= 0) & (kc[None, :] <= qc[:, None])
           & (kc[None, :] >= qc[:, None] - BAND_CHUNKS))
    s = jnp.where(vis, s + bias, -jnp.inf)
    p = jax.nn.softmax(s, axis=-1)
    return jnp.einsum('bhqk,bkhe->bqhe', p.astype(v.dtype), v)


def attn_prompt(q, k, v, rel_bias):
    B, T, _ = q.shape
    qh = q.reshape(B, T, H_B, HD_B)
    kh = k.reshape(B, T, H_B, HD_B)
    vh = v.reshape(B, T, H_B, HD_B)
    pad = ((0, 0), (REACH, 0), (0, 0), (0, 0))
    kp = jnp.pad(kh, pad)
    vp = jnp.pad(vh, pad)

    def one_chunk(c):
        start = c * CHUNK
        qc = lax.dynamic_slice_in_dim(qh, start, CHUNK, axis=1)
        kc = lax.dynamic_slice_in_dim(kp, start, REACH + CHUNK, axis=1)
        vc = lax.dynamic_slice_in_dim(vp, start, REACH + CHUNK, axis=1)
        q_pos = start + jnp.arange(CHUNK, dtype=jnp.int32)
        k_pos = start - REACH + jnp.arange(REACH + CHUNK, dtype=jnp.int32)
        return band_attend(qc, kc, vc, q_pos, k_pos, rel_bias)

    o = lax.map(one_chunk, jnp.arange(T // CHUNK, dtype=jnp.int32))
    o = o.transpose(1, 0, 2, 3, 4).reshape(B, T, W_B)
    keep = min(REACH, T)
    return o, kh[:, T - keep:], vh[:, T - keep:]


def attn_sample(q, k, v, rel_bias, cache_k, cache_v):
    B, T, _ = q.shape
    qh = q.reshape(B, T, H_B, HD_B)
    kh = k.reshape(B, T, H_B, HD_B)
    vh = v.reshape(B, T, H_B, HD_B)
    lc = cache_k.shape[1]
    k_all = jnp.concatenate([cache_k.astype(kh.dtype), kh], axis=1)
    v_all = jnp.concatenate([cache_v.astype(vh.dtype), vh], axis=1)
    k_pos = jnp.concatenate([PAST_LEN - lc + jnp.arange(lc, dtype=jnp.int32),
                             PAST_LEN + jnp.arange(T, dtype=jnp.int32)])
    q_pos = PAST_LEN + jnp.arange(T, dtype=jnp.int32)
    o = band_attend(qh, k_all, v_all, q_pos, k_pos, rel_bias)
    return o.reshape(B, T, W_B), kh, vh


def conv_ffn(h, conv_s0, w_up, conv_w, conv_b, w_down):
    T = h.shape[1]
    a, g = jnp.split(h @ w_up, [D_FF], axis=-1)
    gx = jnp.concatenate([conv_s0.astype(g.dtype), g], axis=1)
    gc = conv_b + sum(conv_w[i] * gx[:, i:i + T] for i in range(CONV_W))
    y = (jax.nn.gelu(gc) * a) @ w_down
    return y, gx[:, T:]


def layer_forward(x, pe, attn_fn, gla_s0, gla_block, conv_s0, w):
    (g_pre_mix, w_in, w_a2, b_a2, g_gla, rel_bias, w_br_a, w_br_b, w_out, g_post_mix,
     g_pre_ffn, w_up, conv_w, conv_b, w_down, g_post_ffn,
     g_pre_ple, w_ple_gate, w_ple, g_post_ple) = w
    h = rms_norm(x, g_pre_mix)
    qa, ka, va, ra, alr, qb, kb, vb, ga, gb = split_cols(h @ w_in)
    oa, s_gla = gla_branch(qa, ka, va, ra, alr, w_a2, b_a2, g_gla, gla_s0, gla_block)
    ob, k_rows, v_rows = attn_fn(qb, kb, vb, rel_bias)
    mix = (jax.nn.sigmoid(ga) * (oa @ w_br_a) + jax.nn.sigmoid(gb) * (ob @ w_br_b)) @ w_out
    x = x + rms_norm(mix, g_post_mix)
    f, conv_new = conv_ffn(rms_norm(x, g_pre_ffn), conv_s0, w_up, conv_w, conv_b, w_down)
    x = x + rms_norm(f, g_post_ffn)
    gate = jax.nn.sigmoid(rms_norm(x, g_pre_ple) @ w_ple_gate)
    x = x + rms_norm(gate * (pe @ w_ple), g_post_ple)
    return x, k_rows, v_rows, s_gla, conv_new


def setup_inputs(seed: int = 0) -> dict:
    key = jax.random.key(seed)
    ks = iter(jax.random.split(key, 32))
    nrm = lambda shape, scale: scale * jax.random.normal(next(ks), shape, jnp.float32)
    gain = lambda shape: 1.0 + nrm(shape, 0.01)
    lc = min(REACH, PAST_LEN)
    return {
        'x_prompt': nrm((BATCH, SEQ, D_MODEL), 1.0),
        'x_sample': nrm((DEC_BATCH, DEC_SEQ, D_MODEL), 1.0),
        'cache_attn_k': nrm((DEPTH, DEC_BATCH, lc, H_B, HD_B), 1.0),
        'cache_attn_v': nrm((DEPTH, DEC_BATCH, lc, H_B, HD_B), 1.0),
        'state_gla': nrm((DEPTH, DEC_BATCH, H_A, DK_A, DV_A), 1.0),
        'state_conv': nrm((DEPTH, DEC_BATCH, CONV_W - 1, D_FF), 1.0),
        'p_prompt': nrm((DEPTH, BATCH, SEQ, PLE_DIM), 1.0),
        'p_sample': nrm((DEPTH, DEC_BATCH, DEC_SEQ, PLE_DIM), 1.0),
        'g_pre_mix': gain((DEPTH, D_MODEL)),
        'w_in': nrm((DEPTH, D_MODEL, N_IN), D_MODEL ** -0.5),
        'w_a2': nrm((DEPTH, GATE_RANK, QK_A), GATE_RANK ** -0.5),
        'b_a2': nrm((DEPTH, QK_A), 0.1),
        'g_gla': gain((DEPTH, V_A)),
        'rel_bias': nrm((DEPTH, H_B, 2 * REL_CLIP + 1), 0.1),
        'w_br_a': nrm((DEPTH, V_A, D_MODEL), V_A ** -0.5),
        'w_br_b': nrm((DEPTH, W_B, D_MODEL), W_B ** -0.5),
        'w_out': nrm((DEPTH, D_MODEL, D_MODEL), D_MODEL ** -0.5),
        'g_post_mix': gain((DEPTH, D_MODEL)),
        'g_pre_ffn': gain((DEPTH, D_MODEL)),
        'w_up': nrm((DEPTH, D_MODEL, 2 * D_FF), D_MODEL ** -0.5),
        'conv_w': nrm((DEPTH, CONV_W, D_FF), CONV_W ** -0.5),
        'conv_b': nrm((DEPTH, D_FF), 0.02),
        'w_down': nrm((DEPTH, D_FF, D_MODEL), D_FF ** -0.5),
        'g_post_ffn': gain((DEPTH, D_MODEL)),
        'g_pre_ple': gain((DEPTH, D_MODEL)),
        'w_ple_gate': nrm((DEPTH, D_MODEL, D_MODEL), D_MODEL ** -0.5),
        'w_ple': nrm((DEPTH, PLE_DIM, D_MODEL), PLE_DIM ** -0.5),
        'g_post_ple': gain((DEPTH, D_MODEL)),
    }


def reference(x_prompt, x_sample, cache_attn_k, cache_attn_v, state_gla, state_conv,
              p_prompt, p_sample, g_pre_mix, w_in, w_a2, b_a2, g_gla, rel_bias,
              w_br_a, w_br_b, w_out, g_post_mix, g_pre_ffn, w_up, conv_w, conv_b,
              w_down, g_post_ffn, g_pre_ple, w_ple_gate, w_ple, g_post_ple):
    bp = x_prompt.shape[0]
    ts = x_sample.shape[1]
    yp, ys = x_prompt, x_sample
    kp_l, vp_l, sp_l, cp_l, ks_l, vs_l, ss_l, cs_l = [], [], [], [], [], [], [], []
    for l in range(DEPTH):
        w = (g_pre_mix[l], w_in[l], w_a2[l], b_a2[l], g_gla[l], rel_bias[l], w_br_a[l],
             w_br_b[l], w_out[l], g_post_mix[l], g_pre_ffn[l], w_up[l], conv_w[l],
             conv_b[l], w_down[l], g_post_ffn[l], g_pre_ple[l], w_ple_gate[l], w_ple[l],
             g_post_ple[l])
        yp, kp, vp, sp, cp = layer_forward(
            yp, p_prompt[l], attn_prompt,
            jnp.zeros((bp, H_A, DK_A, DV_A), jnp.float32), CHUNK,
            jnp.zeros((bp, CONV_W - 1, D_FF), yp.dtype), w)
        ys, ks, vs, ss, cs = layer_forward(
            ys, p_sample[l],
            functools.partial(attn_sample, cache_k=cache_attn_k[l], cache_v=cache_attn_v[l]),
            state_gla[l], ts, state_conv[l], w)
        kp_l.append(kp); vp_l.append(vp); sp_l.append(sp); cp_l.append(cp)
        ks_l.append(ks); vs_l.append(vs); ss_l.append(ss); cs_l.append(cs)
    return (yp, ys,
            jnp.stack(kp_l), jnp.stack(vp_l), jnp.stack(sp_l), jnp.stack(cp_l),
            jnp.stack(ks_l), jnp.stack(vs_l), jnp.stack(ss_l), jnp.stack(cs_l))
```

```python
import functools

import numpy as np
import jax
import jax.numpy as jnp
from jax import lax
from jax.experimental import pallas as pl
from jax.experimental.pallas import tpu as pltpu

CHUNK = 64
H_A, DK_A, DV_A = 4, 128, 256
GATE_RANK = 16
GATE_TAU = 16.0
H_B, HD_B = 8, 64
BAND_CHUNKS = 8
REL_CLIP = 128
CONV_W = 3
PAST_LEN = 4096
EPS = 1e-6
REACH = BAND_CHUNKS * CHUNK
QK_A = H_A * DK_A
V_A = H_A * DV_A
W_B = H_B * HD_B

LANES = 128
VMEM_LIMIT_BYTES = 56 * 1024 * 1024
NEG = -1e30

F32 = jnp.float32
BF16 = jnp.bfloat16


def _dot(a, b):
    return jnp.dot(a, b, preferred_element_type=F32)


def _dot_nt(a, b):
    return lax.dot_general(a, b, (((1,), (1,)), ((), ())), preferred_element_type=F32)


def _dot_tn(a, b):
    return lax.dot_general(a, b, (((0,), (0,)), ((), ())), preferred_element_type=F32)


def _rms(x, g):
    return x * lax.rsqrt(jnp.mean(x * x, axis=-1, keepdims=True) + EPS) * g


def _sigmoid(x):
    return 1.0 / (1.0 + jnp.exp(-x))


def _const_spec(shape):
    nd = len(shape)
    return pl.BlockSpec(shape, lambda *_: (0,) * nd, pipeline_mode=pl.Buffered(1))


def _params(sem):
    return pltpu.CompilerParams(dimension_semantics=sem, vmem_limit_bytes=VMEM_LIMIT_BYTES)


_C_QA, _C_KA, _C_VA, _C_RA = 0, QK_A, 2 * QK_A, 2 * QK_A + V_A
_C_QB = 2 * QK_A + 2 * V_A
_C_KB, _C_VB, _C_GA = _C_QB + W_B, _C_QB + 2 * W_B, _C_QB + 3 * W_B


def _in_proj_kernel(x_ref, g_ref, w_ref, wa2_ref, ba2_ref,
                    qa_ref, ka_ref, va_ref, ra_ref, la_ref, qb_ref, kb_ref, vb_ref,
                    ga_ref, gb_ref, kt_ref, vt_ref, *, tiles_per_seq, d_model):
    c_gb = _C_GA + d_model
    c_al = c_gb + d_model
    h = _rms(x_ref[...], g_ref[...]).astype(BF16)

    def proj(lo, hi):
        return _dot(h, w_ref[:, lo:hi])

    qa_ref[...] = proj(_C_QA, _C_KA).astype(BF16)
    ka_ref[...] = proj(_C_KA, _C_VA).astype(BF16)
    va_ref[...] = proj(_C_VA, _C_RA).astype(BF16)
    ra_ref[...] = proj(_C_RA, _C_QB).astype(BF16)
    qb_ref[...] = proj(_C_QB, _C_KB).astype(BF16)
    kb = proj(_C_KB, _C_VB)
    vb = proj(_C_VB, _C_GA)
    kb_ref[...] = kb.astype(BF16)
    vb_ref[...] = vb.astype(BF16)
    ga_ref[...] = proj(_C_GA, c_gb).astype(BF16)
    gb_ref[...] = proj(c_gb, c_al).astype(BF16)

    @pl.when(pl.program_id(0) % tiles_per_seq == tiles_per_seq - 1)
    def _():
        kt_ref[0] = kb
        vt_ref[0] = vb

    alr = proj(c_al, c_al + LANES).astype(BF16)
    logit = _dot(alr, wa2_ref[...]) + ba2_ref[...]
    ls = -(jnp.maximum(-logit, 0.0) + jnp.log(1.0 + jnp.exp(-jnp.abs(logit))))
    la_ref[...] = ls * (1.0 / GATE_TAU)


def _in_proj(x2d, g_pre, w_re, wa2_p, ba2, *, tm, tiles_per_seq):
    n, d = x2d.shape
    n_tiles = n // tm
    n_seq = n_tiles // tiles_per_seq
    row = lambda width: pl.BlockSpec((tm, width), lambda i: (i, 0))
    tail = pl.BlockSpec((1, tm, W_B), lambda i: (i // tiles_per_seq, 0, 0))
    sd = jax.ShapeDtypeStruct
    outs = [
        (sd((n, QK_A), BF16), row(QK_A)),
        (sd((n, QK_A), BF16), row(QK_A)),
        (sd((n, V_A), BF16), row(V_A)),
        (sd((n, V_A), BF16), row(V_A)),
        (sd((n, QK_A), F32), row(QK_A)),
        (sd((n, W_B), BF16), row(W_B)),
        (sd((n, W_B), BF16), row(W_B)),
        (sd((n, W_B), BF16), row(W_B)),
        (sd((n, d), BF16), row(d)),
        (sd((n, d), BF16), row(d)),
        (sd((n_seq, tm, W_B), F32), tail),
        (sd((n_seq, tm, W_B), F32), tail),
    ]
    return pl.pallas_call(
        functools.partial(_in_proj_kernel, tiles_per_seq=tiles_per_seq, d_model=d),
        grid=(n_tiles,),
        in_specs=[row(d), _const_spec((1, d)), _const_spec(w_re.shape),
                  _const_spec(wa2_p.shape), _const_spec((1, QK_A))],
        out_specs=[o[1] for o in outs],
        out_shape=[o[0] for o in outs],
        compiler_params=_params(("arbitrary",)),
        name="in_proj",
    )(x2d, g_pre, w_re, wa2_p, ba2)


def _gla_tables(L):
    nlev = int(np.log2(L))
    assert 1 << nlev == L
    idx = np.arange(L)
    u = idx[None, :]
    i = idx[:, None]
    groups, masks = [], []
    for t in range(nlev):
        s = 1 << t
        start = (i >> t) << t
        upper = ((i >> t) & 1) == 1
        groups.append(np.where(upper, (u >= start) & (u <= i), (u > i) & (u <= start + s - 1)))
        masks.append(upper & (((u >> t) & 1) == 0) & ((i >> (t + 1)) == (u >> (t + 1))))
    groups.append(u <= i)
    groups.append(u > i)
    masks.append(u == i)
    seg = np.concatenate(groups, axis=0).astype(np.float32)
    msk = np.stack(masks, axis=0).astype(np.float32)
    return nlev, seg, msk


def _gla_kernel(q_ref, k_ref, v_ref, r_ref, la_ref, seg_ref, msk_ref, g_ref, s0_ref,
                o_ref, sout_ref, st_ref, *, L, nlev, n_chunks):
    t_idx = pl.program_id(1)

    @pl.when(t_idx == 0)
    def _():
        st_ref[...] = s0_ref[0]

    rows = lax.broadcasted_iota(jnp.int32, (L, DK_A), 0)
    upper = [((rows >> t) & 1) == 1 for t in range(nlev)]
    scale = DK_A ** -0.5

    def chunk(c, carry):
        r0 = pl.multiple_of(c * L, L)
        a = la_ref[pl.ds(r0, L), :]
        a_hi = a.astype(BF16)
        a_lo = (a - a_hi.astype(F32)).astype(BF16)
        seg = seg_ref[...]
        e_all = jnp.exp(_dot(seg, a_hi) + _dot(seg, a_lo))
        for h in range(H_A):
            ck = slice(h * DK_A, (h + 1) * DK_A)
            cv = slice(h * DV_A, (h + 1) * DV_A)
            q = q_ref[pl.ds(r0, L), ck].astype(F32) * scale
            k = k_ref[pl.ds(r0, L), ck].astype(F32)
            v = v_ref[pl.ds(r0, L), cv]
            att = msk_ref[nlev] * _dot_nt(q.astype(BF16), k.astype(BF16))
            for t in range(nlev):
                x = (jnp.where(upper[t], q, k) * e_all[t * L:(t + 1) * L, ck]).astype(BF16)
                att = att + msk_ref[t] * _dot_nt(x, x)
            e_pre = e_all[nlev * L:(nlev + 1) * L, ck]
            e_suf = e_all[(nlev + 1) * L:(nlev + 2) * L, ck]
            st = st_ref[h]
            o = _dot(att.astype(BF16), v) + _dot_nt((q * e_pre).astype(BF16), st.astype(BF16))
            st_ref[h] = st * e_pre[L - 1:L, :] + _dot_tn(v, (k * e_suf).astype(BF16))
            o = o * lax.rsqrt(jnp.mean(o * o, axis=-1, keepdims=True) + EPS) * g_ref[:, cv]
            r = r_ref[pl.ds(r0, L), cv].astype(F32)
            o_ref[pl.ds(r0, L), cv] = (o * (r * _sigmoid(r))).astype(BF16)
        return carry

    lax.fori_loop(0, n_chunks, chunk, 0)

    @pl.when(t_idx == pl.num_programs(1) - 1)
    def _():
        sout_ref[0] = st_ref[...]


def _gla(qa, ka, va, ra, la, g_gla, s0t, *, n_seq, seq_len, tt, L):
    nlev, seg, msk = _gla_tables(L)
    tps = seq_len // tt
    row = lambda width: pl.BlockSpec((tt, width), lambda b, t: (b * tps + t, 0))
    st_spec = pl.BlockSpec((1, H_A, DV_A, DK_A), lambda b, t: (b, 0, 0, 0))
    n = qa.shape[0]
    return pl.pallas_call(
        functools.partial(_gla_kernel, L=L, nlev=nlev, n_chunks=tt // L),
        grid=(n_seq, tps),
        in_specs=[row(QK_A), row(QK_A), row(V_A), row(V_A), row(QK_A),
                  _const_spec(seg.shape), _const_spec(msk.shape), _const_spec((1, V_A)), st_spec],
        out_specs=[row(V_A), st_spec],
        out_shape=[jax.ShapeDtypeStruct((n, V_A), BF16),
                   jax.ShapeDtypeStruct((n_seq, H_A, DV_A, DK_A), F32)],
        scratch_shapes=[pltpu.VMEM((H_A, DV_A, DK_A), F32)],
        compiler_params=_params(("arbitrary", "arbitrary")),
        name="gla",
    )(qa, ka, va, ra, la, jnp.asarray(seg, BF16), jnp.asarray(msk), g_gla, s0t)


def _attn_kernel(*refs, widths, shifts):
    nb = len(widths)
    q_ref = refs[0]
    k_refs = refs[1:1 + nb]
    v_refs = refs[1 + nb:1 + 2 * nb]
    bias_ref = refs[1 + 2 * nb]
    o_ref = refs[2 + 2 * nb]
    t_idx = pl.program_id(1)
    tq = q_ref.shape[0]
    offs = np.concatenate([[0], np.cumsum(widths)]).tolist()
    pens = [None if s is None else jnp.where(t_idx + s >= 0, 0.0, NEG).astype(F32) for s in shifts]
    low = lax.broadcasted_iota(jnp.int32, (tq, LANES), 1) < HD_B

    for hp in range(H_B // 2):
        cs = slice(hp * LANES, (hp + 1) * LANES)
        q2 = q_ref[:, cs].astype(F32) * (HD_B ** -0.5)
        ks = [k_refs[j][:, cs].astype(BF16) for j in range(nb)]
        vs = [v_refs[j][:, cs].astype(BF16) for j in range(nb)]
        outs = []
        for e in range(2):
            h = 2 * hp + e
            qm = jnp.where(low if e == 0 else ~low, q2, 0.0).astype(BF16)
            ss = []
            for j in range(nb):
                s = _dot_nt(qm, ks[j]) + bias_ref[h, :, offs[j]:offs[j + 1]]
                if pens[j] is not None:
                    s = s + pens[j]
                ss.append(s)
            m = functools.reduce(jnp.maximum, [jnp.max(s, axis=-1, keepdims=True) for s in ss])
            ps = [jnp.exp(s - m) for s in ss]
            l = functools.reduce(jnp.add, [jnp.sum(p, axis=-1, keepdims=True) for p in ps])
            o = functools.reduce(jnp.add, [_dot(ps[j].astype(BF16), vs[j]) for j in range(nb)])
            outs.append(o / l)
        o_ref[:, cs] = jnp.where(low, outs[0], outs[1]).astype(BF16)


def _attn(q, kv_blocks, bias, *, n_seq, tq, tiles_per_seq, shifts):
    widths = tuple(b[2] for b in kv_blocks)
    k_specs = [pl.BlockSpec((b[2], W_B), b[3]) for b in kv_blocks]
    n = q.shape[0]
    return pl.pallas_call(
        functools.partial(_attn_kernel, widths=widths, shifts=tuple(shifts)),
        grid=(n_seq, tiles_per_seq),
        in_specs=[pl.BlockSpec((tq, W_B), lambda b, t: (b * tiles_per_seq + t, 0))]
                 + k_specs + k_specs + [_const_spec(bias.shape)],
        out_specs=pl.BlockSpec((tq, W_B), lambda b, t: (b * tiles_per_seq + t, 0)),
        out_shape=jax.ShapeDtypeStruct((n, W_B), BF16),
        compiler_params=_params(("arbitrary", "arbitrary")),
        name="band_attn",
    )(q, *[b[0] for b in kv_blocks], *[b[1] for b in kv_blocks], bias)


def _band_bias(rel_bias, q_pos, k_pos):
    rel = np.clip(q_pos[:, None] - k_pos[None, :], -REL_CLIP, REL_CLIP) + REL_CLIP
    qc = q_pos // CHUNK
    kc = k_pos // CHUNK
    vis = (kc[None, :] <= qc[:, None]) & (kc[None, :] >= qc[:, None] - BAND_CHUNKS)
    return jnp.where(vis[None], rel_bias.astype(F32)[:, rel], NEG)


def _post_kernel(x_ref, oa_ref, ob_ref, ga_ref, gb_ref, pe_ref, cs0_ref,
                 wbra_ref, wbrb_ref, wout_ref, gpm_ref, gpf_ref, wup_ref, cw_ref, cb_ref,
                 wdn_ref, gqf_ref, gpp_ref, wpg_ref, wple_ref, gqp_ref,
                 y_ref, cso_ref, carry_ref, acc_ref, *, d_ff, fc):
    t_idx = pl.program_id(1)
    tm = x_ref.shape[0]

    @pl.when(t_idx == 0)
    def _():
        carry_ref[...] = cs0_ref[0]

    mix = (_sigmoid(ga_ref[...].astype(F32)) * _dot(oa_ref[...], wbra_ref[...])
           + _sigmoid(gb_ref[...].astype(F32)) * _dot(ob_ref[...], wbrb_ref[...]))
    x1 = x_ref[...] + _rms(_dot(mix.astype(BF16), wout_ref[...]), gpm_ref[...])
    h2 = _rms(x1, gpf_ref[...]).astype(BF16)

    rows = lax.broadcasted_iota(jnp.int32, (tm, fc), 0)
    for c in range(d_ff // fc):
        cc = slice(c * fc, (c + 1) * fc)
        a = _dot(h2, wup_ref[:, cc])
        g = _dot(h2, wup_ref[:, d_ff + c * fc:d_ff + (c + 1) * fc])
        p2 = carry_ref[0:1, cc]
        p1 = carry_ref[1:2, cc]
        g1 = jnp.where(rows == 0, p1, pltpu.roll(g, 1, axis=0))
        g2 = jnp.where(rows == 0, p2, jnp.where(rows == 1, p1, pltpu.roll(g, 2, axis=0)))
        gc = cb_ref[:, cc] + cw_ref[0:1, cc] * g2 + cw_ref[1:2, cc] * g1 + cw_ref[2:3, cc] * g
        carry_ref[:, cc] = g[tm - (CONV_W - 1):, :]
        u = (jax.nn.gelu(gc) * a).astype(BF16)
        d = _dot(u, wdn_ref[cc, :])
        if c == 0:
            acc_ref[...] = d
        else:
            acc_ref[...] += d

    x2 = x1 + _rms(acc_ref[...], gqf_ref[...])
    gate = _sigmoid(_dot(_rms(x2, gpp_ref[...]).astype(BF16), wpg_ref[...]))
    pw = _dot(pe_ref[...].astype(BF16), wple_ref[...])
    y_ref[...] = x2 + _rms(gate * pw, gqp_ref[...])

    @pl.when(t_idx == pl.num_programs(1) - 1)
    def _():
        cso_ref[0] = carry_ref[...]


def _post(x2d, oa, ob, ga, gb, pe2d, cs0, w, *, n_seq, seq_len, tm, fc):
    n, d = x2d.shape
    d_ff = w["wdn"].shape[0]
    tps = seq_len // tm
    row = lambda width: pl.BlockSpec((tm, width), lambda b, t: (b * tps + t, 0))
    cs_spec = pl.BlockSpec((1, CONV_W - 1, d_ff), lambda b, t: (b, 0, 0))
    names = ["wbra", "wbrb", "wout", "gpm", "gpf", "wup", "cw", "cb", "wdn", "gqf", "gpp", "wpg", "wple", "gqp"]
    return pl.pallas_call(
        functools.partial(_post_kernel, d_ff=d_ff, fc=fc),
        grid=(n_seq, tps),
        in_specs=[row(d), row(V_A), row(W_B), row(d), row(d), row(pe2d.shape[1]), cs_spec]
                 + [_const_spec(w[k].shape) for k in names],
        out_specs=[row(d), cs_spec],
        out_shape=[jax.ShapeDtypeStruct((n, d), F32),
                   jax.ShapeDtypeStruct((n_seq, CONV_W - 1, d_ff), F32)],
        scratch_shapes=[pltpu.VMEM((CONV_W - 1, d_ff), F32), pltpu.VMEM((tm, d), F32)],
        compiler_params=_params(("arbitrary", "arbitrary")),
        name="post",
    )(x2d, oa, ob, ga, gb, pe2d, cs0, *[w[k] for k in names])


def _layer(x, pe, s0, conv0, w, *, attn_cache, rel_bias, gla_block, tm_in, tq, tt, tm_post):
    b, t, d = x.shape
    n = b * t
    x2d = x.reshape(n, d)
    keep = min(REACH, t)
    if attn_cache is None:
        assert tm_in == keep and t % tm_in == 0
        tiles_per_seq = t // tm_in
    else:
        assert tm_in == n and keep == t
        tiles_per_seq = 1
    qa, ka, va, ra, la, qb, kb, vb, ga, gb, k_new, v_new = _in_proj(
        x2d, w["g_pre_mix"], w["w_in"], w["w_a2"], w["b_a2"], tm=tm_in, tiles_per_seq=tiles_per_seq)
    k_rows = k_new.reshape(b, keep, H_B, HD_B)
    v_rows = v_new.reshape(b, keep, H_B, HD_B)

    s0t = jnp.swapaxes(s0, -1, -2)
    oa, st = _gla(qa, ka, va, ra, la, w["g_gla"], s0t, n_seq=b, seq_len=t, tt=tt, L=gla_block)
    s_new = jnp.swapaxes(st, -1, -2)

    if attn_cache is None:
        assert t % tq == 0 and tq % CHUNK == 0 and REACH % tq == 0
        tps = t // tq
        nback = REACH // tq
        q_pos = np.arange(tq)
        k_pos = np.arange(-REACH, tq)
        bias = _band_bias(rel_bias, q_pos, k_pos)
        blocks, shifts = [], []
        for j in range(nback + 1):
            sh = j - nback
            imap = functools.partial(lambda bb, tt_, sh_: (bb * tps + jnp.maximum(tt_ + sh_, 0), 0), sh_=sh)
            blocks.append((kb, vb, tq, imap))
            shifts.append(sh if sh < 0 else None)
        ob = _attn(qb, blocks, bias, n_seq=b, tq=tq, tiles_per_seq=tps, shifts=shifts)
    else:
        cache_k, cache_v = attn_cache
        lc = cache_k.shape[1]
        q_pos = PAST_LEN + np.arange(t)
        k_pos = np.concatenate([PAST_LEN - lc + np.arange(lc), PAST_LEN + np.arange(t)])
        bias = _band_bias(rel_bias, q_pos, k_pos)
        bias = jnp.where((k_pos >= 0)[None, None, :], bias, NEG)
        blocks = [(cache_k.reshape(b * lc, W_B), cache_v.reshape(b * lc, W_B), lc, lambda bb, tt_: (bb, 0)),
                  (kb, vb, t, lambda bb, tt_: (bb, 0))]
        ob = _attn(qb, blocks, bias, n_seq=b, tq=t, tiles_per_seq=1, shifts=[None, None])

    y, conv_new = _post(x2d, oa, ob, ga, gb, pe.reshape(n, pe.shape[-1]), conv0, w,
                        n_seq=b, seq_len=t, tm=tm_post, fc=256)
    return y.reshape(b, t, d), k_rows, v_rows, s_new, conv_new


def _prep_weights(g_pre_mix, w_in, w_a2, b_a2, g_gla, w_br_a, w_br_b, w_out, g_post_mix, g_pre_ffn,
                  w_up, conv_w, conv_b, w_down, g_post_ffn, g_pre_ple, w_ple_gate, w_ple, g_post_ple):
    d = w_in.shape[0]
    sizes = (QK_A, QK_A, V_A, V_A, GATE_RANK, W_B, W_B, W_B, d, d)
    offs = np.concatenate([[0], np.cumsum(sizes)])
    cols = [w_in[:, offs[i]:offs[i + 1]] for i in range(len(sizes))]
    order = [0, 1, 2, 3, 5, 6, 7, 8, 9, 4]
    w_re = jnp.concatenate([cols[i] for i in order]
                           + [jnp.zeros((d, LANES - GATE_RANK), w_in.dtype)], axis=1).astype(BF16)
    wa2_p = jnp.concatenate([w_a2, jnp.zeros((LANES - GATE_RANK, QK_A), w_a2.dtype)], axis=0).astype(BF16)
    r1 = lambda v: v.reshape(1, -1).astype(F32)
    return dict(
        g_pre_mix=r1(g_pre_mix), w_in=w_re, w_a2=wa2_p, b_a2=r1(b_a2), g_gla=r1(g_gla),
        wbra=w_br_a.astype(BF16), wbrb=w_br_b.astype(BF16), wout=w_out.astype(BF16),
        gpm=r1(g_post_mix), gpf=r1(g_pre_ffn), wup=w_up.astype(BF16), cw=conv_w.astype(F32),
        cb=r1(conv_b), wdn=w_down.astype(BF16), gqf=r1(g_post_ffn), gpp=r1(g_pre_ple),
        wpg=w_ple_gate.astype(BF16), wple=w_ple.astype(BF16), gqp=r1(g_post_ple))


def kernel(x_prompt, x_sample, cache_attn_k, cache_attn_v, state_gla, state_conv, p_prompt, p_sample,
           g_pre_mix, w_in, w_a2, b_a2, g_gla, rel_bias, w_br_a, w_br_b, w_out, g_post_mix, g_pre_ffn,
           w_up, conv_w, conv_b, w_down, g_post_ffn, g_pre_ple, w_ple_gate, w_ple, g_post_ple):
    depth = w_in.shape[0]
    bp, tp, _ = x_prompt.shape
    bs, ts, _ = x_sample.shape
    d_ff = w_down.shape[1]
    yp, ys = x_prompt, x_sample
    outs = [[] for _ in range(8)]
    for l in range(depth):
        w = _prep_weights(g_pre_mix[l], w_in[l], w_a2[l], b_a2[l], g_gla[l], w_br_a[l], w_br_b[l],
                          w_out[l], g_post_mix[l], g_pre_ffn[l], w_up[l], conv_w[l], conv_b[l],
                          w_down[l], g_post_ffn[l], g_pre_ple[l], w_ple_gate[l], w_ple[l], g_post_ple[l])
        yp, kp, vp, sp, cp = _layer(
            yp, p_prompt[l], jnp.zeros((bp, H_A, DK_A, DV_A), F32), jnp.zeros((bp, CONV_W - 1, d_ff), F32), w,
            attn_cache=None, rel_bias=rel_bias[l], gla_block=CHUNK,
            tm_in=min(REACH, tp), tq=min(256, tp), tt=min(512, tp), tm_post=min(256, tp))
        ys, ks, vs, ss, cs = _layer(
            ys, p_sample[l], state_gla[l], state_conv[l], w,
            attn_cache=(cache_attn_k[l], cache_attn_v[l]), rel_bias=rel_bias[l], gla_block=ts,
            tm_in=bs * ts, tq=ts, tt=ts, tm_post=ts)
        for lst, val in zip(outs, (kp, vp, sp, cp, ks, vs, ss, cs)):
            lst.append(val)
    return (yp, ys) + tuple(jnp.stack(o) for o in outs)
```

```python
import functools

import numpy as np
import jax
import jax.numpy as jnp
from jax import lax
from jax.experimental import pallas as pl
from jax.experimental.pallas import tpu as pltpu

CHUNK = 64
H_A, DK_A, DV_A = 4, 128, 256
GATE_RANK = 16
GATE_TAU = 16.0
H_B, HD_B = 8, 64
BAND_CHUNKS = 8
REL_CLIP = 128
CONV_W = 3
PAST_LEN = 4096
EPS = 1e-6
REACH = BAND_CHUNKS * CHUNK
QK_A = H_A * DK_A
V_A = H_A * DV_A
W_B = H_B * HD_B

LANES = 128
VMEM_LIMIT_BYTES = 56 * 1024 * 1024
NEG = -1e30
LOG2E = 1.4426950408889634

F32 = jnp.float32
BF16 = jnp.bfloat16


def _dot(a, b):
    return jnp.dot(a, b, preferred_element_type=F32)


def _dot_nt(a, b):
    return lax.dot_general(a, b, (((1,), (1,)), ((), ())), preferred_element_type=F32)


def _dot_tn(a, b):
    return lax.dot_general(a, b, (((0,), (0,)), ((), ())), preferred_element_type=F32)


def _rms(x, g):
    return x * lax.rsqrt(jnp.mean(x * x, axis=-1, keepdims=True) + EPS) * g


def _sigmoid(x):
    return 1.0 / (1.0 + jnp.exp(-x))


def _const_spec(shape):
    nd = len(shape)
    return pl.BlockSpec(shape, lambda *_: (0,) * nd, pipeline_mode=pl.Buffered(1))


def _params(sem):
    return pltpu.CompilerParams(dimension_semantics=sem, vmem_limit_bytes=VMEM_LIMIT_BYTES)


_C_QA, _C_KA, _C_VA, _C_RA = 0, QK_A, 2 * QK_A, 2 * QK_A + V_A
_C_QB = 2 * QK_A + 2 * V_A
_C_KB, _C_VB, _C_GA = _C_QB + W_B, _C_QB + 2 * W_B, _C_QB + 3 * W_B


def _in_proj_kernel(x_ref, g_ref, w_ref, wa2_ref, ba2_ref,
                    qa_ref, ka_ref, va_ref, ra_ref, la_ref, qb_ref, kb_ref, vb_ref,
                    ga_ref, gb_ref, kt_ref, vt_ref, *, tiles_per_seq, d_model):
    c_gb = _C_GA + d_model
    c_al = c_gb + d_model
    h = _rms(x_ref[...], g_ref[...]).astype(BF16)

    def proj(lo, hi):
        return _dot(h, w_ref[:, lo:hi])

    qa_ref[...] = proj(_C_QA, _C_KA).astype(BF16)
    ka_ref[...] = proj(_C_KA, _C_VA).astype(BF16)
    va_ref[...] = proj(_C_VA, _C_RA).astype(BF16)
    ra_ref[...] = proj(_C_RA, _C_QB).astype(BF16)
    qb_ref[...] = proj(_C_QB, _C_KB).astype(BF16)
    kb = proj(_C_KB, _C_VB)
    vb = proj(_C_VB, _C_GA)
    kb_ref[...] = kb.astype(BF16)
    vb_ref[...] = vb.astype(BF16)
    ga_ref[...] = proj(_C_GA, c_gb).astype(BF16)
    gb_ref[...] = proj(c_gb, c_al).astype(BF16)

    @pl.when(pl.program_id(0) % tiles_per_seq == tiles_per_seq - 1)
    def _():
        kt_ref[0] = kb
        vt_ref[0] = vb

    alr = proj(c_al, c_al + LANES).astype(BF16)
    logit = _dot(alr, wa2_ref[...]) + ba2_ref[...]
    ls = -(jnp.maximum(-logit, 0.0) + jnp.log(1.0 + jnp.exp(-jnp.abs(logit))))
    la_ref[...] = ls * (1.0 / GATE_TAU)


def _in_proj(x2d, g_pre, w_re, wa2_p, ba2, *, tm, tiles_per_seq):
    n, d = x2d.shape
    n_tiles = n // tm
    n_seq = n_tiles // tiles_per_seq
    row = lambda width: pl.BlockSpec((tm, width), lambda i: (i, 0))
    tail = pl.BlockSpec((1, tm, W_B), lambda i: (i // tiles_per_seq, 0, 0))
    sd = jax.ShapeDtypeStruct
    outs = [
        (sd((n, QK_A), BF16), row(QK_A)),
        (sd((n, QK_A), BF16), row(QK_A)),
        (sd((n, V_A), BF16), row(V_A)),
        (sd((n, V_A), BF16), row(V_A)),
        (sd((n, QK_A), F32), row(QK_A)),
        (sd((n, W_B), BF16), row(W_B)),
        (sd((n, W_B), BF16), row(W_B)),
        (sd((n, W_B), BF16), row(W_B)),
        (sd((n, d), BF16), row(d)),
        (sd((n, d), BF16), row(d)),
        (sd((n_seq, tm, W_B), F32), tail),
        (sd((n_seq, tm, W_B), F32), tail),
    ]
    return pl.pallas_call(
        functools.partial(_in_proj_kernel, tiles_per_seq=tiles_per_seq, d_model=d),
        grid=(n_tiles,),
        in_specs=[row(d), _const_spec((1, d)), _const_spec(w_re.shape),
                  _const_spec(wa2_p.shape), _const_spec((1, QK_A))],
        out_specs=[o[1] for o in outs],
        out_shape=[o[0] for o in outs],
        compiler_params=_params(("arbitrary",)),
        name="in_proj",
    )(x2d, g_pre, w_re, wa2_p, ba2)


def _gla_tables(L):
    nlev = int(np.log2(L))
    assert 1 << nlev == L
    idx = np.arange(L)
    u = idx[None, :]
    i = idx[:, None]
    groups, masks = [], []
    for t in range(nlev):
        s = 1 << t
        start = (i >> t) << t
        upper = ((i >> t) & 1) == 1
        groups.append(np.where(upper, (u >= start) & (u <= i), (u > i) & (u <= start + s - 1)))
        masks.append(upper & (((u >> t) & 1) == 0) & ((i >> (t + 1)) == (u >> (t + 1))))
    groups.append(u <= i)
    groups.append(u > i)
    masks.append(u == i)
    seg = np.concatenate(groups, axis=0).astype(np.float32)
    msk = np.stack(masks, axis=0).astype(np.float32)
    return nlev, seg, msk


def _gla_kernel(q_ref, k_ref, v_ref, r_ref, la_ref, seg_ref, msk_ref, g_ref, s0_ref,
                o_ref, sout_ref, st_ref, *, L, nlev, n_chunks, nb):
    t_idx = pl.program_id(1)

    @pl.when(t_idx == 0)
    def _():
        st_ref[...] = s0_ref[...]

    rows = lax.broadcasted_iota(jnp.int32, (L, DK_A), 0)
    upper = [((rows >> t) & 1) == 1 for t in range(nlev)]
    scale = DK_A ** -0.5
    streams = [(s, h) for s in range(nb) for h in range(H_A)]

    def chunk(c, carry):
        rs = pl.ds(pl.multiple_of(c * L, L), L)
        ck = lambda h: slice(h * DK_A, (h + 1) * DK_A)
        cv = lambda h: slice(h * DV_A, (h + 1) * DV_A)
        load_q = lambda s, h: q_ref[s, rs, ck(h)].astype(F32) * scale
        load_k = lambda s, h: k_ref[s, rs, ck(h)].astype(F32)

        e_all = []
        for s in range(nb):
            a = la_ref[s, rs, :]
            a_hi = a.astype(BF16)
            a_lo = (a - a_hi.astype(F32)).astype(BF16)
            seg = seg_ref[...]
            e_all.append(jnp.exp(_dot(seg, a_hi) + _dot(seg, a_lo)))

        att = {}
        for s, h in streams:
            q, k = load_q(s, h), load_k(s, h)
            acc = msk_ref[nlev] * _dot_nt(q.astype(BF16), k.astype(BF16))
            for t in range(nlev):
                x = (jnp.where(upper[t], q, k) * e_all[s][t * L:(t + 1) * L, ck(h)]).astype(BF16)
                acc = acc + msk_ref[t] * _dot_nt(x, x)
            att[s, h] = acc.astype(BF16)

        outs = {}
        for s, h in streams:
            e_pre = e_all[s][nlev * L:(nlev + 1) * L, ck(h)]
            e_suf = e_all[s][(nlev + 1) * L:(nlev + 2) * L, ck(h)]
            v = v_ref[s, rs, cv(h)]
            st = st_ref[s, h]
            outs[s, h] = (_dot(att[s, h], v)
                          + _dot_nt((load_q(s, h) * e_pre).astype(BF16), st.astype(BF16)))
            st_ref[s, h] = st * e_pre[L - 1:L, :] + _dot_tn(v, (load_k(s, h) * e_suf).astype(BF16))

        for s, h in streams:
            o = outs[s, h]
            o = o * lax.rsqrt(jnp.mean(o * o, axis=-1, keepdims=True) + EPS) * g_ref[:, cv(h)]
            r = r_ref[s, rs, cv(h)].astype(F32)
            o_ref[s, rs, cv(h)] = (o * (r * _sigmoid(r))).astype(BF16)
        return carry

    lax.fori_loop(0, n_chunks, chunk, 0)

    @pl.when(t_idx == pl.num_programs(1) - 1)
    def _():
        sout_ref[...] = st_ref[...]


def _gla(qa, ka, va, ra, la, g_gla, s0t, *, n_seq, seq_len, tt, L, nb):
    nlev, seg, msk = _gla_tables(L)
    assert n_seq % nb == 0 and seq_len % tt == 0 and tt % L == 0
    r3 = lambda x: x.reshape(n_seq, seq_len, x.shape[-1])
    row = lambda width: pl.BlockSpec((nb, tt, width), lambda b, t: (b, t, 0))
    st_spec = pl.BlockSpec((nb, H_A, DV_A, DK_A), lambda b, t: (b, 0, 0, 0))
    oa, st = pl.pallas_call(
        functools.partial(_gla_kernel, L=L, nlev=nlev, n_chunks=tt // L, nb=nb),
        grid=(n_seq // nb, seq_len // tt),
        in_specs=[row(QK_A), row(QK_A), row(V_A), row(V_A), row(QK_A),
                  _const_spec(seg.shape), _const_spec(msk.shape), _const_spec((1, V_A)), st_spec],
        out_specs=[row(V_A), st_spec],
        out_shape=[jax.ShapeDtypeStruct((n_seq, seq_len, V_A), BF16),
                   jax.ShapeDtypeStruct((n_seq, H_A, DV_A, DK_A), F32)],
        scratch_shapes=[pltpu.VMEM((nb, H_A, DV_A, DK_A), F32)],
        compiler_params=_params(("arbitrary", "arbitrary")),
        name="gla",
    )(r3(qa), r3(ka), r3(va), r3(ra), r3(la), jnp.asarray(seg, BF16), jnp.asarray(msk), g_gla, s0t)
    return oa.reshape(n_seq * seq_len, V_A), st


def _attn_kernel(*refs, widths, shifts):
    nb = len(widths)
    q_ref = refs[0]
    k_refs = refs[1:1 + nb]
    v_refs = refs[1 + nb:1 + 2 * nb]
    bias_ref = refs[1 + 2 * nb]
    o_ref = refs[2 + 2 * nb]
    t_idx = pl.program_id(1)
    tq = q_ref.shape[0]
    offs = np.concatenate([[0], np.cumsum(widths)]).tolist()
    same_width = len(set(widths)) == 1
    low = lax.broadcasted_iota(jnp.int32, (tq, LANES), 1) < HD_B

    def lanes(h):
        return slice((h // 2) * LANES, (h // 2 + 1) * LANES)

    def scores(h):
        q2 = q_ref[:, lanes(h)].astype(F32) * (HD_B ** -0.5 * LOG2E)
        qm = jnp.where(low if h % 2 == 0 else ~low, q2, 0.0).astype(BF16)
        return [_dot_nt(qm, k_refs[j][:, lanes(h)].astype(BF16)) for j in range(nb)]

    def lane_reduce(xs, op, red):
        if same_width:
            return red(functools.reduce(op, xs), axis=-1, keepdims=True)
        return functools.reduce(op, [red(x, axis=-1, keepdims=True) for x in xs])

    def body(pens):
        nxt = scores(0)
        prev = None
        for h in range(H_B):
            raw = nxt
            if h + 1 < H_B:
                nxt = scores(h + 1)
            ss = []
            for j in range(nb):
                s = raw[j] + bias_ref[h, :, offs[j]:offs[j + 1]]
                ss.append(s if pens[j] is None else s + pens[j])
            m = lane_reduce(ss, jnp.maximum, jnp.max)
            ps = [jnp.exp2(s - m) for s in ss]
            l = lane_reduce(ps, jnp.add, jnp.sum)
            o = functools.reduce(jnp.add, [_dot(ps[j].astype(BF16), v_refs[j][:, lanes(h)].astype(BF16))
                                           for j in range(nb)])
            o = o / l
            if h % 2 == 0:
                prev = o
            else:
                o_ref[:, lanes(h)] = jnp.where(low, prev, o).astype(BF16)

    n_early = max([-s for s in shifts if s is not None], default=0)
    if n_early == 0:
        body([None] * nb)
    else:
        @pl.when(t_idx >= n_early)
        def _():
            body([None] * nb)

        @pl.when(t_idx < n_early)
        def _():
            body([None if s is None else jnp.where(t_idx + s >= 0, 0.0, NEG).astype(F32) for s in shifts])


def _attn(q, kv_blocks, bias, *, n_seq, tq, tiles_per_seq, shifts):
    widths = tuple(b[2] for b in kv_blocks)
    k_specs = [pl.BlockSpec((b[2], W_B), b[3]) for b in kv_blocks]
    n = q.shape[0]
    return pl.pallas_call(
        functools.partial(_attn_kernel, widths=widths, shifts=tuple(shifts)),
        grid=(n_seq, tiles_per_seq),
        in_specs=[pl.BlockSpec((tq, W_B), lambda b, t: (b * tiles_per_seq + t, 0))]
                 + k_specs + k_specs + [_const_spec(bias.shape)],
        out_specs=pl.BlockSpec((tq, W_B), lambda b, t: (b * tiles_per_seq + t, 0)),
        out_shape=jax.ShapeDtypeStruct((n, W_B), BF16),
        compiler_params=_params(("arbitrary", "arbitrary")),
        name="band_attn",
    )(q, *[b[0] for b in kv_blocks], *[b[1] for b in kv_blocks], bias)


def _band_bias(rel_bias, q_pos, k_pos):
    tq, nk = len(q_pos), len(k_pos)
    assert np.all(np.diff(q_pos) == 1) and np.all(np.diff(k_pos) == 1)
    period = nk + tq
    m = np.arange(period)
    m = np.where(m < nk, m, m - period)
    d = (q_pos[0] - k_pos[0]) - m
    row = rel_bias.astype(F32)[:, np.clip(d, -REL_CLIP, REL_CLIP) + REL_CLIP]
    tile = jnp.tile(row, (1, tq))[:, :tq * (period - 1)].reshape(-1, tq, period - 1)[:, :, :nk]
    qc = q_pos // CHUNK
    kc = k_pos // CHUNK
    vis = (kc[None, :] <= qc[:, None]) & (kc[None, :] >= qc[:, None] - BAND_CHUNKS)
    return jnp.where(vis[None], tile * LOG2E, NEG)


def _post_kernel(x_ref, oa_ref, ob_ref, ga_ref, gb_ref, pe_ref, cs0_ref,
                 wbra_ref, wbrb_ref, wout_ref, gpm_ref, gpf_ref, wup_ref, cw_ref, cb_ref,
                 wdn_ref, gqf_ref, gpp_ref, wpg_ref, wple_ref, gqp_ref,
                 y_ref, cso_ref, carry_ref, acc_ref, *, d_ff, fc):
    t_idx = pl.program_id(1)
    tm = x_ref.shape[0]

    @pl.when(t_idx == 0)
    def _():
        carry_ref[...] = cs0_ref[0]

    pw = _dot(pe_ref[...].astype(BF16), wple_ref[...])
    mix = (_sigmoid(ga_ref[...].astype(F32)) * _dot(oa_ref[...], wbra_ref[...])
           + _sigmoid(gb_ref[...].astype(F32)) * _dot(ob_ref[...], wbrb_ref[...]))
    x1 = x_ref[...] + _rms(_dot(mix.astype(BF16), wout_ref[...]), gpm_ref[...])
    h2 = _rms(x1, gpf_ref[...]).astype(BF16)

    rows = lax.broadcasted_iota(jnp.int32, (tm, fc), 0)
    n_fc = d_ff // fc

    def up(c):
        return (_dot(h2, wup_ref[:, c * fc:(c + 1) * fc]),
                _dot(h2, wup_ref[:, d_ff + c * fc:d_ff + (c + 1) * fc]))

    nxt = up(0)
    for c in range(n_fc):
        cc = slice(c * fc, (c + 1) * fc)
        a, g = nxt
        if c + 1 < n_fc:
            nxt = up(c + 1)
        p2 = carry_ref[0:1, cc]
        p1 = carry_ref[1:2, cc]
        g1 = jnp.where(rows == 0, p1, pltpu.roll(g, 1, axis=0))
        g2 = jnp.where(rows == 0, p2, jnp.where(rows == 1, p1, pltpu.roll(g, 2, axis=0)))
        gc = cb_ref[:, cc] + cw_ref[0:1, cc] * g2 + cw_ref[1:2, cc] * g1 + cw_ref[2:3, cc] * g
        carry_ref[:, cc] = g[tm - (CONV_W - 1):, :]
        u = (jax.nn.gelu(gc) * a).astype(BF16)
        d = _dot(u, wdn_ref[cc, :])
        if c == 0:
            acc_ref[...] = d
        else:
            acc_ref[...] += d

    x2 = x1 + _rms(acc_ref[...], gqf_ref[...])
    gate = _sigmoid(_dot(_rms(x2, gpp_ref[...]).astype(BF16), wpg_ref[...]))
    y_ref[...] = x2 + _rms(gate * pw, gqp_ref[...])

    @pl.when(t_idx == pl.num_programs(1) - 1)
    def _():
        cso_ref[0] = carry_ref[...]


def _post(x2d, oa, ob, ga, gb, pe2d, cs0, w, *, n_seq, seq_len, tm, fc):
    n, d = x2d.shape
    d_ff = w["wdn"].shape[0]
    tps = seq_len // tm
    row = lambda width: pl.BlockSpec((tm, width), lambda b, t: (b * tps + t, 0))
    cs_spec = pl.BlockSpec((1, CONV_W - 1, d_ff), lambda b, t: (b, 0, 0))
    names = ["wbra", "wbrb", "wout", "gpm", "gpf", "wup", "cw", "cb", "wdn", "gqf", "gpp", "wpg", "wple", "gqp"]
    return pl.pallas_call(
        functools.partial(_post_kernel, d_ff=d_ff, fc=fc),
        grid=(n_seq, tps),
        in_specs=[row(d), row(V_A), row(W_B), row(d), row(d), row(pe2d.shape[1]), cs_spec]
                 + [_const_spec(w[k].shape) for k in names],
        out_specs=[row(d), cs_spec],
        out_shape=[jax.ShapeDtypeStruct((n, d), F32),
                   jax.ShapeDtypeStruct((n_seq, CONV_W - 1, d_ff), F32)],
        scratch_shapes=[pltpu.VMEM((CONV_W - 1, d_ff), F32), pltpu.VMEM((tm, d), F32)],
        compiler_params=_params(("arbitrary", "arbitrary")),
        name="post",
    )(x2d, oa, ob, ga, gb, pe2d, cs0, *[w[k] for k in names])


def _layer(x, pe, s0, conv0, w, *, attn_cache, rel_bias, gla_block, tm_in, tq, tt, tm_post):
    b, t, d = x.shape
    n = b * t
    x2d = x.reshape(n, d)
    keep = min(REACH, t)
    if attn_cache is None:
        assert tm_in == keep and t % tm_in == 0
        tiles_per_seq = t // tm_in
    else:
        assert tm_in == n and keep == t
        tiles_per_seq = 1
    qa, ka, va, ra, la, qb, kb, vb, ga, gb, k_new, v_new = _in_proj(
        x2d, w["g_pre_mix"], w["w_in"], w["w_a2"], w["b_a2"], tm=tm_in, tiles_per_seq=tiles_per_seq)
    k_rows = k_new.reshape(b, keep, H_B, HD_B)
    v_rows = v_new.reshape(b, keep, H_B, HD_B)

    s0t = jnp.swapaxes(s0, -1, -2)
    oa, st = _gla(qa, ka, va, ra, la, w["g_gla"], s0t, n_seq=b, seq_len=t, tt=tt, L=gla_block,
                  nb=2 if b % 2 == 0 else 1)
    s_new = jnp.swapaxes(st, -1, -2)

    if attn_cache is None:
        assert t % tq == 0 and tq % CHUNK == 0 and REACH % tq == 0
        tps = t // tq
        nback = REACH // tq
        q_pos = np.arange(tq)
        k_pos = np.arange(-REACH, tq)
        bias = _band_bias(rel_bias, q_pos, k_pos)
        blocks, shifts = [], []
        for j in range(nback + 1):
            sh = j - nback
            imap = functools.partial(lambda bb, tt_, sh_: (bb * tps + jnp.maximum(tt_ + sh_, 0), 0), sh_=sh)
            blocks.append((kb, vb, tq, imap))
            shifts.append(sh if sh < 0 else None)
        ob = _attn(qb, blocks, bias, n_seq=b, tq=tq, tiles_per_seq=tps, shifts=shifts)
    else:
        cache_k, cache_v = attn_cache
        lc = cache_k.shape[1]
        q_pos = PAST_LEN + np.arange(t)
        k_pos = np.concatenate([PAST_LEN - lc + np.arange(lc), PAST_LEN + np.arange(t)])
        bias = _band_bias(rel_bias, q_pos, k_pos)
        bias = jnp.where((k_pos >= 0)[None, None, :], bias, NEG)
        blocks = [(cache_k.reshape(b * lc, W_B), cache_v.reshape(b * lc, W_B), lc, lambda bb, tt_: (bb, 0)),
                  (kb, vb, t, lambda bb, tt_: (bb, 0))]
        ob = _attn(qb, blocks, bias, n_seq=b, tq=t, tiles_per_seq=1, shifts=[None, None])

    y, conv_new = _post(x2d, oa, ob, ga, gb, pe.reshape(n, pe.shape[-1]), conv0, w,
                        n_seq=b, seq_len=t, tm=tm_post, fc=256)
    return y.reshape(b, t, d), k_rows, v_rows, s_new, conv_new


def _prep_weights(g_pre_mix, w_in, w_a2, b_a2, g_gla, w_br_a, w_br_b, w_out, g_post_mix, g_pre_ffn,
                  w_up, conv_w, conv_b, w_down, g_post_ffn, g_pre_ple, w_ple_gate, w_ple, g_post_ple):
    d = w_in.shape[0]
    sizes = (QK_A, QK_A, V_A, V_A, GATE_RANK, W_B, W_B, W_B, d, d)
    offs = np.concatenate([[0], np.cumsum(sizes)])
    cols = [w_in[:, offs[i]:offs[i + 1]] for i in range(len(sizes))]
    order = [0, 1, 2, 3, 5, 6, 7, 8, 9, 4]
    w_re = jnp.concatenate([cols[i] for i in order]
                           + [jnp.zeros((d, LANES - GATE_RANK), w_in.dtype)], axis=1).astype(BF16)
    wa2_p = jnp.concatenate([w_a2, jnp.zeros((LANES - GATE_RANK, QK_A), w_a2.dtype)], axis=0).astype(BF16)
    r1 = lambda v: v.reshape(1, -1).astype(F32)
    return dict(
        g_pre_mix=r1(g_pre_mix), w_in=w_re, w_a2=wa2_p, b_a2=r1(b_a2), g_gla=r1(g_gla),
        wbra=w_br_a.astype(BF16), wbrb=w_br_b.astype(BF16), wout=w_out.astype(BF16),
        gpm=r1(g_post_mix), gpf=r1(g_pre_ffn), wup=w_up.astype(BF16), cw=conv_w.astype(F32),
        cb=r1(conv_b), wdn=w_down.astype(BF16), gqf=r1(g_post_ffn), gpp=r1(g_pre_ple),
        wpg=w_ple_gate.astype(BF16), wple=w_ple.astype(BF16), gqp=r1(g_post_ple))


def kernel(x_prompt, x_sample, cache_attn_k, cache_attn_v, state_gla, state_conv, p_prompt, p_sample,
           g_pre_mix, w_in, w_a2, b_a2, g_gla, rel_bias, w_br_a, w_br_b, w_out, g_post_mix, g_pre_ffn,
           w_up, conv_w, conv_b, w_down, g_post_ffn, g_pre_ple, w_ple_gate, w_ple, g_post_ple):
    depth = w_in.shape[0]
    bp, tp, _ = x_prompt.shape
    bs, ts, _ = x_sample.shape
    d_ff = w_down.shape[1]
    yp, ys = x_prompt, x_sample
    outs = [[] for _ in range(8)]
    for l in range(depth):
        w = _prep_weights(g_pre_mix[l], w_in[l], w_a2[l], b_a2[l], g_gla[l], w_br_a[l], w_br_b[l],
                          w_out[l], g_post_mix[l], g_pre_ffn[l], w_up[l], conv_w[l], conv_b[l],
                          w_down[l], g_post_ffn[l], g_pre_ple[l], w_ple_gate[l], w_ple[l], g_post_ple[l])
        yp, kp, vp, sp, cp = _layer(
            yp, p_prompt[l], jnp.zeros((bp, H_A, DK_A, DV_A), F32), jnp.zeros((bp, CONV_W - 1, d_ff), F32), w,
            attn_cache=None, rel_bias=rel_bias[l], gla_block=CHUNK,
            tm_in=min(REACH, tp), tq=min(256, tp), tt=min(512, tp), tm_post=min(256, tp))
        ys, ks, vs, ss, cs = _layer(
            ys, p_sample[l], state_gla[l], state_conv[l], w,
            attn_cache=(cache_attn_k[l], cache_attn_v[l]), rel_bias=rel_bias[l], gla_block=ts,
            tm_in=bs * ts, tq=ts, tt=ts, tm_post=ts)
        for lst, val in zip(outs, (kp, vp, sp, cp, ks, vs, ss, cs)):
            lst.append(val)
    return (yp, ys) + tuple(jnp.stack(o) for o in outs)
```

```python
import functools

import numpy as np
import jax
import jax.numpy as jnp
from jax import lax
from jax.experimental import pallas as pl
from jax.experimental.pallas import tpu as pltpu

CHUNK = 64
H_A, DK_A, DV_A = 4, 128, 256
GATE_RANK = 16
GATE_TAU = 16.0
H_B, HD_B = 8, 64
BAND_CHUNKS = 8
REL_CLIP = 128
CONV_W = 3
PAST_LEN = 4096
EPS = 1e-6
REACH = BAND_CHUNKS * CHUNK
QK_A = H_A * DK_A
V_A = H_A * DV_A
W_B = H_B * HD_B

LANES = 128
SUBLANES = 8
VMEM_LIMIT_BYTES = 56 * 1024 * 1024
NEG = -1e30
LOG2E = 1.4426950408889634

F32 = jnp.float32
BF16 = jnp.bfloat16


def _dot(a, b):
    return jnp.dot(a, b, preferred_element_type=F32)


def _dot_nt(a, b):
    return lax.dot_general(a, b, (((1,), (1,)), ((), ())), preferred_element_type=F32)


def _dot_tn(a, b):
    return lax.dot_general(a, b, (((0,), (0,)), ((), ())), preferred_element_type=F32)


def _rms(x, g):
    return x * lax.rsqrt(jnp.mean(x * x, axis=-1, keepdims=True) + EPS) * g


def _sigmoid(x):
    return 1.0 / (1.0 + jnp.exp(-x))


def _const_spec(shape):
    nd = len(shape)
    return pl.BlockSpec(shape, lambda *_: (0,) * nd, pipeline_mode=pl.Buffered(1))


def _params(sem):
    return pltpu.CompilerParams(dimension_semantics=sem, vmem_limit_bytes=VMEM_LIMIT_BYTES)


_C_QA, _C_KA, _C_VA, _C_RA = 0, QK_A, 2 * QK_A, 2 * QK_A + V_A
_C_QB = 2 * QK_A + 2 * V_A
_C_KB, _C_VB, _C_GA = _C_QB + W_B, _C_QB + 2 * W_B, _C_QB + 3 * W_B


def _in_proj_kernel(x_ref, g_ref, w_ref, wa2_ref, ba2_ref,
                    qa_ref, ka_ref, va_ref, ra_ref, la_ref, qb_ref, kb_ref, vb_ref,
                    ga_ref, gb_ref, kt_ref, vt_ref, *, d_model):
    c_gb = _C_GA + d_model
    c_al = c_gb + d_model
    h = _rms(x_ref[...], g_ref[...]).astype(BF16)

    def proj(lo, hi):
        return _dot(h, w_ref[:, lo:hi])

    alr = proj(c_al, c_al + LANES).astype(BF16)
    qa_ref[...] = proj(_C_QA, _C_KA).astype(BF16)
    ka_ref[...] = proj(_C_KA, _C_VA).astype(BF16)
    logit = _dot(alr, wa2_ref[...]) + ba2_ref[...]
    va_ref[...] = proj(_C_VA, _C_RA).astype(BF16)
    ls = -(jnp.maximum(-logit, 0.0) + jnp.log(1.0 + jnp.exp(-jnp.abs(logit))))
    la_ref[...] = ls * (1.0 / GATE_TAU)
    ra_ref[...] = proj(_C_RA, _C_QB).astype(BF16)
    qb_ref[...] = proj(_C_QB, _C_KB).astype(BF16)
    kb = proj(_C_KB, _C_VB)
    kb_ref[...] = kb.astype(BF16)
    kt_ref[0] = kb
    vb = proj(_C_VB, _C_GA)
    vb_ref[...] = vb.astype(BF16)
    vt_ref[0] = vb
    ga_ref[...] = proj(_C_GA, c_gb).astype(BF16)
    gb_ref[...] = proj(c_gb, c_al).astype(BF16)


def _in_proj(x2d, g_pre, w_re, wa2_p, ba2, *, tm, tiles_per_seq):
    n, d = x2d.shape
    n_tiles = n // tm
    n_seq = n_tiles // tiles_per_seq
    row = lambda width: pl.BlockSpec((tm, width), lambda i: (i, 0))
    tail = pl.BlockSpec((1, tm, W_B), lambda i: (i // tiles_per_seq, 0, 0))
    sd = jax.ShapeDtypeStruct
    outs = [
        (sd((n, QK_A), BF16), row(QK_A)),
        (sd((n, QK_A), BF16), row(QK_A)),
        (sd((n, V_A), BF16), row(V_A)),
        (sd((n, V_A), BF16), row(V_A)),
        (sd((n, QK_A), F32), row(QK_A)),
        (sd((n, W_B), BF16), row(W_B)),
        (sd((n, W_B), BF16), row(W_B)),
        (sd((n, W_B), BF16), row(W_B)),
        (sd((n, d), BF16), row(d)),
        (sd((n, d), BF16), row(d)),
        (sd((n_seq, tm, W_B), F32), tail),
        (sd((n_seq, tm, W_B), F32), tail),
    ]
    return pl.pallas_call(
        functools.partial(_in_proj_kernel, d_model=d),
        grid=(n_tiles,),
        in_specs=[row(d), _const_spec((1, d)), _const_spec(w_re.shape),
                  _const_spec(wa2_p.shape), _const_spec((1, QK_A))],
        out_specs=[o[1] for o in outs],
        out_shape=[o[0] for o in outs],
        compiler_params=_params(("arbitrary",)),
        name="in_proj",
    )(x2d, g_pre, w_re, wa2_p, ba2)


def _gla_tables(L):
    nlev = int(np.log2(L))
    assert 1 << nlev == L and L % SUBLANES == 0
    n_lo = min(int(np.log2(SUBLANES)), nlev)
    idx = np.arange(L)
    u = idx[None, :]
    i = idx[:, None]
    groups, masks = [], []
    for t in range(nlev):
        s = 1 << t
        start = (i >> t) << t
        upper = ((i >> t) & 1) == 1
        groups.append(np.where(upper, (u >= start) & (u <= i), (u > i) & (u <= start + s - 1)))
        masks.append(upper & (((u >> t) & 1) == 0) & ((i >> (t + 1)) == (u >> (t + 1))))
    groups.append(u <= i)
    seg = np.concatenate(groups, axis=0).astype(np.float32)
    seg2 = np.concatenate([seg, seg], axis=1)
    msk_lo = np.stack(masks[:n_lo] + [u == i], axis=0).astype(np.float32)
    up_blocks, msk_up = [], []
    for t in range(n_lo, nlev):
        ub = [b for b in range(L // SUBLANES) if ((b * SUBLANES) >> t) & 1]
        up_blocks.append(ub)
        msk_up.append(np.concatenate([masks[t][b * SUBLANES:(b + 1) * SUBLANES] for b in ub], axis=0))
    msk_up = np.stack(msk_up, axis=0).astype(np.float32) if msk_up else np.zeros((1, SUBLANES, L), np.float32)
    return nlev, n_lo, seg2, msk_lo, msk_up, up_blocks


def _gla_kernel(q_ref, k_ref, v_ref, r_ref, la_ref, seg_ref, mlo_ref, mup_ref, g_ref, s0_ref,
                o_ref, sout_ref, st_ref, *, L, nlev, n_lo, up_blocks, n_chunks, nb):
    t_idx = pl.program_id(1)
    nblk = L // SUBLANES
    blk = lambda b: slice(b * SUBLANES, (b + 1) * SUBLANES)

    @pl.when(t_idx == 0)
    def _():
        st_ref[...] = s0_ref[...]

    rows = lax.broadcasted_iota(jnp.int32, (L, DK_A), 0)
    upper = [((rows >> t) & 1) == 1 for t in range(nlev)]
    scale = DK_A ** -0.5
    streams = [(s, h) for s in range(nb) for h in range(H_A)]

    def chunk(c, carry):
        rs = pl.ds(pl.multiple_of(c * L, L), L)
        ck = lambda h: slice(h * DK_A, (h + 1) * DK_A)
        cv = lambda h: slice(h * DV_A, (h + 1) * DV_A)
        load_q = lambda s, h: q_ref[s, rs, ck(h)].astype(F32) * scale
        load_k = lambda s, h: k_ref[s, rs, ck(h)].astype(F32)

        e_all, e_suf_all = [], []
        for s in range(nb):
            a = la_ref[s, rs, :]
            a_hi = a.astype(BF16)
            a_lo = (a - a_hi.astype(F32)).astype(BF16)
            z = _dot(seg_ref[...], jnp.concatenate([a_hi, a_lo], axis=0))
            e_all.append(jnp.exp(z))
            z_pre = z[nlev * L:(nlev + 1) * L]
            e_suf_all.append(jnp.exp(z_pre[L - 1:L] - z_pre))

        att = {}
        for s, h in streams:
            q, k = load_q(s, h), load_k(s, h)
            acc = mlo_ref[n_lo] * _dot_nt(q.astype(BF16), k.astype(BF16))
            for t in range(n_lo):
                x = (jnp.where(upper[t], q, k) * e_all[s][t * L:(t + 1) * L, ck(h)]).astype(BF16)
                acc = acc + mlo_ref[t] * _dot_nt(x, x)
            rows8 = [acc[blk(b)] for b in range(nblk)]
            for t in range(n_lo, nlev):
                x = jnp.where(upper[t], q, k) * e_all[s][t * L:(t + 1) * L, ck(h)]
                ub = up_blocks[t - n_lo]
                lhs = jnp.concatenate([x[blk(b)] for b in ub], axis=0).astype(BF16)
                part = mup_ref[t - n_lo] * _dot_nt(lhs, x.astype(BF16))
                for n, b in enumerate(ub):
                    rows8[b] = rows8[b] + part[blk(n)]
            att[s, h] = jnp.concatenate(rows8, axis=0).astype(BF16)

        outs = {}
        for s, h in streams:
            e_pre = e_all[s][nlev * L:(nlev + 1) * L, ck(h)]
            e_suf = e_suf_all[s][:, ck(h)]
            v = v_ref[s, rs, cv(h)]
            st = st_ref[s, h]
            outs[s, h] = (_dot(att[s, h], v)
                          + _dot_nt((load_q(s, h) * e_pre).astype(BF16), st.astype(BF16)))
            st_ref[s, h] = st * e_pre[L - 1:L, :] + _dot_tn(v, (load_k(s, h) * e_suf).astype(BF16))

        for s, h in streams:
            o = outs[s, h]
            o = o * lax.rsqrt(jnp.mean(o * o, axis=-1, keepdims=True) + EPS) * g_ref[:, cv(h)]
            r = r_ref[s, rs, cv(h)].astype(F32)
            o_ref[s, rs, cv(h)] = (o * (r * _sigmoid(r))).astype(BF16)
        return carry

    lax.fori_loop(0, n_chunks, chunk, 0)

    @pl.when(t_idx == pl.num_programs(1) - 1)
    def _():
        sout_ref[...] = st_ref[...]


def _gla(qa, ka, va, ra, la, g_gla, s0t, *, n_seq, seq_len, tt, L, nb):
    nlev, n_lo, seg, msk_lo, msk_up, up_blocks = _gla_tables(L)
    assert n_seq % nb == 0 and seq_len % tt == 0 and tt % L == 0
    r3 = lambda x: x.reshape(n_seq, seq_len, x.shape[-1])
    row = lambda width: pl.BlockSpec((nb, tt, width), lambda b, t: (b, t, 0))
    st_spec = pl.BlockSpec((nb, H_A, DV_A, DK_A), lambda b, t: (b, 0, 0, 0))
    oa, st = pl.pallas_call(
        functools.partial(_gla_kernel, L=L, nlev=nlev, n_lo=n_lo, up_blocks=up_blocks,
                          n_chunks=tt // L, nb=nb),
        grid=(n_seq // nb, seq_len // tt),
        in_specs=[row(QK_A), row(QK_A), row(V_A), row(V_A), row(QK_A), _const_spec(seg.shape),
                  _const_spec(msk_lo.shape), _const_spec(msk_up.shape), _const_spec((1, V_A)), st_spec],
        out_specs=[row(V_A), st_spec],
        out_shape=[jax.ShapeDtypeStruct((n_seq, seq_len, V_A), BF16),
                   jax.ShapeDtypeStruct((n_seq, H_A, DV_A, DK_A), F32)],
        scratch_shapes=[pltpu.VMEM((nb, H_A, DV_A, DK_A), F32)],
        compiler_params=_params(("arbitrary", "arbitrary")),
        name="gla",
    )(r3(qa), r3(ka), r3(va), r3(ra), r3(la), jnp.asarray(seg, BF16), jnp.asarray(msk_lo),
      jnp.asarray(msk_up), g_gla, s0t)
    return oa.reshape(n_seq * seq_len, V_A), st


def _attn_kernel(*refs, widths, shifts):
    nb = len(widths)
    q_ref = refs[0]
    k_refs = refs[1:1 + nb]
    v_refs = refs[1 + nb:1 + 2 * nb]
    bias_ref = refs[1 + 2 * nb]
    o_ref = refs[2 + 2 * nb]
    t_idx = pl.program_id(1)
    tq = q_ref.shape[0]
    offs = np.concatenate([[0], np.cumsum(widths)]).tolist()
    same_width = len(set(widths)) == 1
    low = lax.broadcasted_iota(jnp.int32, (tq, LANES), 1) < HD_B
    low_t = lax.broadcasted_iota(jnp.int32, (LANES, tq), 0) < HD_B

    def lanes(h):
        return slice((h // 2) * LANES, (h // 2 + 1) * LANES)

    def scores(h):
        q2 = q_ref[:, lanes(h)].astype(F32) * (HD_B ** -0.5 * LOG2E)
        qm = jnp.where(low if h % 2 == 0 else ~low, q2, 0.0).astype(BF16)
        return [_dot_nt(k_refs[j][:, lanes(h)].astype(BF16), qm) for j in range(nb)]

    def key_reduce(xs, op, red):
        if same_width:
            return red(functools.reduce(op, xs), axis=0, keepdims=True)
        return functools.reduce(op, [red(x, axis=0, keepdims=True) for x in xs])

    def body(pens):
        def biased(raw, h):
            ss = []
            for j in range(nb):
                s = raw[j] + bias_ref[h, offs[j]:offs[j + 1], :]
                ss.append(s if pens[j] is None else s + pens[j])
            return ss, key_reduce(ss, jnp.maximum, jnp.max)

        raws = {0: scores(0), 1: scores(1), 2: scores(2)}
        staged = {0: biased(raws[0], 0), 1: biased(raws[1], 1)}
        prev = None
        for h in range(H_B):
            if h + 3 < H_B:
                raws[h + 3] = scores(h + 3)
            if h + 2 < H_B:
                staged[h + 2] = biased(raws.pop(h + 2), h + 2)
            ss, m = staged.pop(h)
            ps = [jnp.exp2(s - m) for s in ss]
            l = key_reduce(ps, jnp.add, jnp.sum)
            o_t = functools.reduce(jnp.add, [_dot_tn(v_refs[j][:, lanes(h)].astype(BF16), ps[j].astype(BF16))
                                             for j in range(nb)])
            o_t = o_t / l
            if h % 2 == 0:
                prev = o_t
            else:
                o_ref[:, lanes(h)] = jnp.where(low_t, prev, o_t).T.astype(BF16)

    n_early = max([-s for s in shifts if s is not None], default=0)
    if n_early == 0:
        body([None] * nb)
    else:
        @pl.when(t_idx >= n_early)
        def _():
            body([None] * nb)

        @pl.when(t_idx < n_early)
        def _():
            body([None if s is None else jnp.where(t_idx + s >= 0, 0.0, NEG).astype(F32) for s in shifts])


def _attn(q, kv_blocks, bias, *, n_seq, tq, tiles_per_seq, shifts):
    widths = tuple(b[2] for b in kv_blocks)
    k_specs = [pl.BlockSpec((b[2], W_B), b[3]) for b in kv_blocks]
    n = q.shape[0]
    return pl.pallas_call(
        functools.partial(_attn_kernel, widths=widths, shifts=tuple(shifts)),
        grid=(n_seq, tiles_per_seq),
        in_specs=[pl.BlockSpec((tq, W_B), lambda b, t: (b * tiles_per_seq + t, 0))]
                 + k_specs + k_specs + [_const_spec(bias.shape)],
        out_specs=pl.BlockSpec((tq, W_B), lambda b, t: (b * tiles_per_seq + t, 0)),
        out_shape=jax.ShapeDtypeStruct((n, W_B), BF16),
        compiler_params=_params(("arbitrary", "arbitrary")),
        name="band_attn",
    )(q, *[b[0] for b in kv_blocks], *[b[1] for b in kv_blocks], bias)


def _band_bias(rel_bias, q_pos, k_pos):
    tq, nk = len(q_pos), len(k_pos)
    assert np.all(np.diff(q_pos) == 1) and np.all(np.diff(k_pos) == 1)
    period = nk + tq
    m = np.arange(period)
    m = np.where(m < tq, m, m - period)
    d = (q_pos[0] - k_pos[0]) + m
    row = rel_bias.astype(F32)[:, np.clip(d, -REL_CLIP, REL_CLIP) + REL_CLIP]
    tile = jnp.tile(row, (1, nk))[:, :nk * (period - 1)].reshape(-1, nk, period - 1)[:, :, :tq]
    qc = q_pos // CHUNK
    kc = k_pos // CHUNK
    vis = (kc[:, None] <= qc[None, :]) & (kc[:, None] >= qc[None, :] - BAND_CHUNKS)
    return jnp.where(vis[None], tile * LOG2E, NEG)


def _post_kernel(x_ref, oa_ref, ob_ref, ga_ref, gb_ref, pe_ref, cs0_ref,
                 wbra_ref, wbrb_ref, wout_ref, gpm_ref, gpf_ref, wup_ref, cw_ref, cb_ref,
                 wdn_ref, gqf_ref, gpp_ref, wpg_ref, wple_ref, gqp_ref,
                 y_ref, cso_ref, carry_ref, acc_ref, *, d_ff, fc, n_sub):
    t_idx = pl.program_id(1)
    ts = x_ref.shape[0] // n_sub
    n_fc = d_ff // fc
    rows = lax.broadcasted_iota(jnp.int32, (ts, fc), 0)
    st = [dict() for _ in range(n_sub)]
    rsl = lambda i: slice(i * ts, (i + 1) * ts)

    @pl.when(t_idx == 0)
    def _():
        carry_ref[...] = cs0_ref[0]

    def head1(i):
        r = rsl(i)
        st[i]["pw"] = _dot(pe_ref[r, :].astype(BF16), wple_ref[...])
        st[i]["mix"] = (_sigmoid(ga_ref[r, :].astype(F32)) * _dot(oa_ref[r, :], wbra_ref[...])
                        + _sigmoid(gb_ref[r, :].astype(F32)) * _dot(ob_ref[r, :], wbrb_ref[...])).astype(BF16)

    def head2(i):
        x1 = x_ref[rsl(i), :] + _rms(_dot(st[i].pop("mix"), wout_ref[...]), gpm_ref[...])
        st[i]["x1"] = x1
        st[i]["h2"] = _rms(x1, gpf_ref[...]).astype(BF16)

    def tail1(i):
        x2 = st[i].pop("x1") + _rms(acc_ref[i], gqf_ref[...])
        st[i]["x2"] = x2
        st[i]["hp"] = _rms(x2, gpp_ref[...]).astype(BF16)

    def tail2(i):
        gate = _sigmoid(_dot(st[i].pop("hp"), wpg_ref[...]))
        y_ref[rsl(i), :] = st[i].pop("x2") + _rms(gate * st[i].pop("pw"), gqp_ref[...])

    def ffn(i, hooks):
        h2 = st[i].pop("h2")

        def up(c):
            return (_dot(h2, wup_ref[:, c * fc:(c + 1) * fc]),
                    _dot(h2, wup_ref[:, d_ff + c * fc:d_ff + (c + 1) * fc]))

        nxt = up(0)
        for c in range(n_fc):
            cc = slice(c * fc, (c + 1) * fc)
            a, g = nxt
            if c + 1 < n_fc:
                nxt = up(c + 1)
            if c in hooks:
                hooks[c]()
            p2 = carry_ref[0:1, cc]
            p1 = carry_ref[1:2, cc]
            g1 = jnp.where(rows == 0, p1, pltpu.roll(g, 1, axis=0))
            g2 = jnp.where(rows == 0, p2, jnp.where(rows == 1, p1, pltpu.roll(g, 2, axis=0)))
            gc = cb_ref[:, cc] + cw_ref[0:1, cc] * g2 + cw_ref[1:2, cc] * g1 + cw_ref[2:3, cc] * g
            carry_ref[:, cc] = g[ts - (CONV_W - 1):, :]
            u = (jax.nn.gelu(gc) * a).astype(BF16)
            d = _dot(u, wdn_ref[cc, :])
            if c == 0:
                acc_ref[i] = d
            else:
                acc_ref[i] += d

    head1(0)
    head2(0)
    for i in range(n_sub):
        hooks = {}
        if i + 1 < n_sub:
            hooks[1] = functools.partial(head1, i + 1)
            hooks[3] = functools.partial(head2, i + 1)
        if i >= 1:
            hooks[5] = functools.partial(tail1, i - 1)
            hooks[7] = functools.partial(tail2, i - 1)
        ffn(i, hooks)
    tail1(n_sub - 1)
    tail2(n_sub - 1)

    @pl.when(t_idx == pl.num_programs(1) - 1)
    def _():
        cso_ref[0] = carry_ref[...]


def _post(x2d, oa, ob, ga, gb, pe2d, cs0, w, *, n_seq, seq_len, tm, fc, n_sub):
    n, d = x2d.shape
    d_ff = w["wdn"].shape[0]
    assert seq_len % tm == 0 and tm % n_sub == 0 and d_ff % fc == 0
    assert n_sub == 1 or d_ff // fc >= 8
    tps = seq_len // tm
    row = lambda width: pl.BlockSpec((tm, width), lambda b, t: (b * tps + t, 0))
    cs_spec = pl.BlockSpec((1, CONV_W - 1, d_ff), lambda b, t: (b, 0, 0))
    names = ["wbra", "wbrb", "wout", "gpm", "gpf", "wup", "cw", "cb", "wdn", "gqf", "gpp", "wpg", "wple", "gqp"]
    return pl.pallas_call(
        functools.partial(_post_kernel, d_ff=d_ff, fc=fc, n_sub=n_sub),
        grid=(n_seq, tps),
        in_specs=[row(d), row(V_A), row(W_B), row(d), row(d), row(pe2d.shape[1]), cs_spec]
                 + [_const_spec(w[k].shape) for k in names],
        out_specs=[row(d), cs_spec],
        out_shape=[jax.ShapeDtypeStruct((n, d), F32),
                   jax.ShapeDtypeStruct((n_seq, CONV_W - 1, d_ff), F32)],
        scratch_shapes=[pltpu.VMEM((CONV_W - 1, d_ff), F32), pltpu.VMEM((n_sub, tm // n_sub, d), F32)],
        compiler_params=_params(("arbitrary", "arbitrary")),
        name="post",
    )(x2d, oa, ob, ga, gb, pe2d, cs0, *[w[k] for k in names])


def _prompt_tiles(t):
    post_rows = 256
    post_sub = 2 if t % (2 * post_rows) == 0 else 1
    return dict(gla_block=CHUNK, tm_in=min(REACH, t), tq=min(256, t), tt=min(512, t),
                tm_post=min(post_rows * post_sub, t), post_sub=post_sub)


def _sample_tiles(b, t):
    return dict(gla_block=t, tm_in=b * t, tq=t, tt=t, tm_post=t, post_sub=1)


def _layer(x, pe, s0, conv0, w, *, attn_cache, rel_bias, gla_block, tm_in, tq, tt, tm_post, post_sub):
    b, t, d = x.shape
    n = b * t
    x2d = x.reshape(n, d)
    keep = min(REACH, t)
    if attn_cache is None:
        assert tm_in == keep and t % tm_in == 0
        tiles_per_seq = t // tm_in
    else:
        assert tm_in == n and keep == t
        tiles_per_seq = 1
    qa, ka, va, ra, la, qb, kb, vb, ga, gb, k_new, v_new = _in_proj(
        x2d, w["g_pre_mix"], w["w_in"], w["w_a2"], w["b_a2"], tm=tm_in, tiles_per_seq=tiles_per_seq)
    k_rows = k_new.reshape(b, keep, H_B, HD_B)
    v_rows = v_new.reshape(b, keep, H_B, HD_B)

    s0t = jnp.swapaxes(s0, -1, -2)
    oa, st = _gla(qa, ka, va, ra, la, w["g_gla"], s0t, n_seq=b, seq_len=t, tt=tt, L=gla_block,
                  nb=2 if b % 2 == 0 else 1)
    s_new = jnp.swapaxes(st, -1, -2)

    if attn_cache is None:
        assert t % tq == 0 and tq % CHUNK == 0 and REACH % tq == 0
        tps = t // tq
        nback = REACH // tq
        q_pos = np.arange(tq)
        k_pos = np.arange(-REACH, tq)
        bias = _band_bias(rel_bias, q_pos, k_pos)
        blocks, shifts = [], []
        for j in range(nback + 1):
            sh = j - nback
            imap = functools.partial(lambda bb, tt_, sh_: (bb * tps + jnp.maximum(tt_ + sh_, 0), 0), sh_=sh)
            blocks.append((kb, vb, tq, imap))
            shifts.append(sh if sh < 0 else None)
        ob = _attn(qb, blocks, bias, n_seq=b, tq=tq, tiles_per_seq=tps, shifts=shifts)
    else:
        cache_k, cache_v = attn_cache
        lc = cache_k.shape[1]
        q_pos = PAST_LEN + np.arange(t)
        k_pos = np.concatenate([PAST_LEN - lc + np.arange(lc), PAST_LEN + np.arange(t)])
        bias = _band_bias(rel_bias, q_pos, k_pos)
        bias = jnp.where((k_pos >= 0)[None, :, None], bias, NEG)
        blocks = [(cache_k.reshape(b * lc, W_B), cache_v.reshape(b * lc, W_B), lc, lambda bb, tt_: (bb, 0)),
                  (kb, vb, t, lambda bb, tt_: (bb, 0))]
        ob = _attn(qb, blocks, bias, n_seq=b, tq=t, tiles_per_seq=1, shifts=[None, None])

    y, conv_new = _post(x2d, oa, ob, ga, gb, pe.reshape(n, pe.shape[-1]), conv0, w,
                        n_seq=b, seq_len=t, tm=tm_post, fc=256, n_sub=post_sub)
    return y.reshape(b, t, d), k_rows, v_rows, s_new, conv_new


def _prep_weights(g_pre_mix, w_in, w_a2, b_a2, g_gla, w_br_a, w_br_b, w_out, g_post_mix, g_pre_ffn,
                  w_up, conv_w, conv_b, w_down, g_post_ffn, g_pre_ple, w_ple_gate, w_ple, g_post_ple):
    d = w_in.shape[0]
    sizes = (QK_A, QK_A, V_A, V_A, GATE_RANK, W_B, W_B, W_B, d, d)
    offs = np.concatenate([[0], np.cumsum(sizes)])
    cols = [w_in[:, offs[i]:offs[i + 1]] for i in range(len(sizes))]
    order = [0, 1, 2, 3, 5, 6, 7, 8, 9, 4]
    w_re = jnp.concatenate([cols[i] for i in order]
                           + [jnp.zeros((d, LANES - GATE_RANK), w_in.dtype)], axis=1).astype(BF16)
    wa2_p = jnp.concatenate([w_a2, jnp.zeros((LANES - GATE_RANK, QK_A), w_a2.dtype)], axis=0).astype(BF16)
    r1 = lambda v: v.reshape(1, -1).astype(F32)
    return dict(
        g_pre_mix=r1(g_pre_mix), w_in=w_re, w_a2=wa2_p, b_a2=r1(b_a2), g_gla=r1(g_gla),
        wbra=w_br_a.astype(BF16), wbrb=w_br_b.astype(BF16), wout=w_out.astype(BF16),
        gpm=r1(g_post_mix), gpf=r1(g_pre_ffn), wup=w_up.astype(BF16), cw=conv_w.astype(F32),
        cb=r1(conv_b), wdn=w_down.astype(BF16), gqf=r1(g_post_ffn), gpp=r1(g_pre_ple),
        wpg=w_ple_gate.astype(BF16), wple=w_ple.astype(BF16), gqp=r1(g_post_ple))


def kernel(x_prompt, x_sample, cache_attn_k, cache_attn_v, state_gla, state_conv, p_prompt, p_sample,
           g_pre_mix, w_in, w_a2, b_a2, g_gla, rel_bias, w_br_a, w_br_b, w_out, g_post_mix, g_pre_ffn,
           w_up, conv_w, conv_b, w_down, g_post_ffn, g_pre_ple, w_ple_gate, w_ple, g_post_ple):
    depth = w_in.shape[0]
    bp, tp, _ = x_prompt.shape
    bs, ts, _ = x_sample.shape
    d_ff = w_down.shape[1]
    yp, ys = x_prompt, x_sample
    outs = [[] for _ in range(8)]
    for l in range(depth):
        w = _prep_weights(g_pre_mix[l], w_in[l], w_a2[l], b_a2[l], g_gla[l], w_br_a[l], w_br_b[l],
                          w_out[l], g_post_mix[l], g_pre_ffn[l], w_up[l], conv_w[l], conv_b[l],
                          w_down[l], g_post_ffn[l], g_pre_ple[l], w_ple_gate[l], w_ple[l], g_post_ple[l])
        yp, kp, vp, sp, cp = _layer(
            yp, p_prompt[l], jnp.zeros((bp, H_A, DK_A, DV_A), F32), jnp.zeros((bp, CONV_W - 1, d_ff), F32), w,
            attn_cache=None, rel_bias=rel_bias[l], **_prompt_tiles(tp))
        ys, ks, vs, ss, cs = _layer(
            ys, p_sample[l], state_gla[l], state_conv[l], w,
            attn_cache=(cache_attn_k[l], cache_attn_v[l]), rel_bias=rel_bias[l], **_sample_tiles(bs, ts))
        for lst, val in zip(outs, (kp, vp, sp, cp, ks, vs, ss, cs)):
            lst.append(val)
    return (yp, ys) + tuple(jnp.stack(o) for o in outs)
```

```python
import functools

import numpy as np
import jax
import jax.numpy as jnp
from jax import lax
from jax.experimental import pallas as pl
from jax.experimental.pallas import tpu as pltpu

CHUNK = 64
H_A, DK_A, DV_A = 4, 128, 256
GATE_RANK = 16
GATE_TAU = 16.0
H_B, HD_B = 8, 64
BAND_CHUNKS = 8
REL_CLIP = 128
CONV_W = 3
PAST_LEN = 4096
EPS = 1e-6
REACH = BAND_CHUNKS * CHUNK
QK_A = H_A * DK_A
V_A = H_A * DV_A
W_B = H_B * HD_B

LANES = 128
SUBLANES = 8
VMEM_LIMIT_BYTES = 56 * 1024 * 1024
NEG = -1e30
LOG2E = 1.4426950408889634

F32 = jnp.float32
BF16 = jnp.bfloat16


def _dot(a, b):
    return jnp.dot(a, b, preferred_element_type=F32)


def _dot_nt(a, b):
    return lax.dot_general(a, b, (((1,), (1,)), ((), ())), preferred_element_type=F32)


def _dot_tn(a, b):
    return lax.dot_general(a, b, (((0,), (0,)), ((), ())), preferred_element_type=F32)


def _rms(x, g):
    return x * lax.rsqrt(jnp.mean(x * x, axis=-1, keepdims=True) + EPS) * g


def _sigmoid(x):
    return 1.0 / (1.0 + jnp.exp(-x))


def _const_spec(shape):
    nd = len(shape)
    return pl.BlockSpec(shape, lambda *_: (0,) * nd, pipeline_mode=pl.Buffered(1))


def _params(sem):
    return pltpu.CompilerParams(dimension_semantics=sem, vmem_limit_bytes=VMEM_LIMIT_BYTES)


_A_QA, _A_KA, _A_VA, _A_RA, _A_END = 0, QK_A, 2 * QK_A, 2 * QK_A + V_A, 2 * QK_A + 2 * V_A
_B_QB, _B_KB, _B_VB, _B_GA = 0, W_B, 2 * W_B, 3 * W_B


def _in_proj_kernel(x_ref, g_ref, wa_ref, wal_ref, wb_ref, wa2_ref, ba2_ref,
                    qa_ref, ka_ref, va_ref, ra_ref, la_ref, qb_ref, kb_ref, vb_ref,
                    ga_ref, gb_ref, kt_ref, vt_ref, *, d_model):
    b_gb = _B_GA + d_model
    h = _rms(x_ref[...], g_ref[...]).astype(BF16)
    proj_a = lambda lo, hi: _dot(h, wa_ref[:, lo:hi])
    proj_b = lambda lo, hi: _dot(h, wb_ref[:, lo:hi])

    alr = _dot(h, wal_ref[...]).astype(BF16)
    qa_ref[...] = proj_a(_A_QA, _A_KA).astype(BF16)
    ka_ref[...] = proj_a(_A_KA, _A_VA).astype(BF16)
    logit = _dot(alr, wa2_ref[...]) + ba2_ref[...]
    va_ref[...] = proj_a(_A_VA, _A_RA).astype(BF16)
    ls = -(jnp.maximum(-logit, 0.0) + jnp.log(1.0 + jnp.exp(-jnp.abs(logit))))
    la_ref[...] = ls * (1.0 / GATE_TAU)
    ra_ref[...] = proj_a(_A_RA, _A_END).astype(BF16)
    qb_ref[...] = proj_b(_B_QB, _B_KB).astype(BF16)
    kb = proj_b(_B_KB, _B_VB)
    kb_ref[...] = kb.astype(BF16)
    kt_ref[0] = kb
    vb = proj_b(_B_VB, _B_GA)
    vb_ref[...] = vb.astype(BF16)
    vt_ref[0] = vb
    ga_ref[...] = proj_b(_B_GA, b_gb).astype(BF16)
    gb_ref[...] = proj_b(b_gb, b_gb + d_model).astype(BF16)


def _in_proj(x2d, g_pre, w_parts, wa2_p, ba2, *, tm, tiles_per_seq):
    n, d = x2d.shape
    n_tiles = n // tm
    n_seq = n_tiles // tiles_per_seq
    row = lambda width: pl.BlockSpec((tm, width), lambda i: (i, 0))
    tail = pl.BlockSpec((1, tm, W_B), lambda i: (i // tiles_per_seq, 0, 0))
    sd = jax.ShapeDtypeStruct
    outs = [
        (sd((n, QK_A), BF16), row(QK_A)),
        (sd((n, QK_A), BF16), row(QK_A)),
        (sd((n, V_A), BF16), row(V_A)),
        (sd((n, V_A), BF16), row(V_A)),
        (sd((n, QK_A), F32), row(QK_A)),
        (sd((n, W_B), BF16), row(W_B)),
        (sd((n, W_B), BF16), row(W_B)),
        (sd((n, W_B), BF16), row(W_B)),
        (sd((n, d), BF16), row(d)),
        (sd((n, d), BF16), row(d)),
        (sd((n_seq, tm, W_B), F32), tail),
        (sd((n_seq, tm, W_B), F32), tail),
    ]
    return pl.pallas_call(
        functools.partial(_in_proj_kernel, d_model=d),
        grid=(n_tiles,),
        in_specs=[row(d), _const_spec((1, d))] + [_const_spec(p.shape) for p in w_parts]
                 + [_const_spec(wa2_p.shape), _const_spec((1, QK_A))],
        out_specs=[o[1] for o in outs],
        out_shape=[o[0] for o in outs],
        compiler_params=_params(("arbitrary",)),
        name="in_proj",
    )(x2d, g_pre, *w_parts, wa2_p, ba2)


def _gla_tables(L):
    nlev = int(np.log2(L))
    assert 1 << nlev == L and L % SUBLANES == 0
    n_lo = min(int(np.log2(SUBLANES)), nlev)
    idx = np.arange(L)
    u = idx[None, :]
    i = idx[:, None]
    groups, masks = [], []
    for t in range(nlev):
        s = 1 << t
        start = (i >> t) << t
        upper = ((i >> t) & 1) == 1
        groups.append(np.where(upper, (u >= start) & (u <= i), (u > i) & (u <= start + s - 1)))
        masks.append(upper & (((u >> t) & 1) == 0) & ((i >> (t + 1)) == (u >> (t + 1))))
    groups.append(u <= i)
    seg = np.concatenate(groups, axis=0).astype(np.float32)
    seg2 = np.concatenate([seg, seg], axis=1)
    msk_lo = np.stack(masks[:n_lo] + [u == i], axis=0).astype(np.float32)
    up_blocks, msk_up = [], []
    for t in range(n_lo, nlev):
        ub = [b for b in range(L // SUBLANES) if ((b * SUBLANES) >> t) & 1]
        up_blocks.append(ub)
        msk_up.append(np.concatenate([masks[t][b * SUBLANES:(b + 1) * SUBLANES] for b in ub], axis=0))
    msk_up = np.stack(msk_up, axis=0).astype(np.float32) if msk_up else np.zeros((1, SUBLANES, L), np.float32)
    return nlev, n_lo, seg2, msk_lo, msk_up, up_blocks


def _gla_kernel(q_ref, k_ref, v_ref, r_ref, la_ref, seg_ref, mlo_ref, mup_ref, g_ref, s0_ref,
                o_ref, sout_ref, st_ref, *, L, nlev, n_lo, up_blocks, n_chunks, nb):
    t_idx = pl.program_id(1)
    nblk = L // SUBLANES
    blk = lambda b: slice(b * SUBLANES, (b + 1) * SUBLANES)

    @pl.when(t_idx == 0)
    def _():
        st_ref[...] = s0_ref[...]

    rows = lax.broadcasted_iota(jnp.int32, (L, DK_A), 0)
    upper = [((rows >> t) & 1) == 1 for t in range(nlev)]
    scale = DK_A ** -0.5
    streams = [(s, h) for s in range(nb) for h in range(H_A)]

    def chunk(c, carry):
        rs = pl.ds(pl.multiple_of(c * L, L), L)
        ck = lambda h: slice(h * DK_A, (h + 1) * DK_A)
        cv = lambda h: slice(h * DV_A, (h + 1) * DV_A)
        load_q = lambda s, h: q_ref[s, rs, ck(h)].astype(F32) * scale
        load_k = lambda s, h: k_ref[s, rs, ck(h)].astype(F32)

        e_all, e_suf_all = [], []
        for s in range(nb):
            a = la_ref[s, rs, :]
            a_hi = a.astype(BF16)
            a_lo = (a - a_hi.astype(F32)).astype(BF16)
            z = _dot(seg_ref[...], jnp.concatenate([a_hi, a_lo], axis=0))
            e_all.append(jnp.exp(z))
            z_pre = z[nlev * L:(nlev + 1) * L]
            e_suf_all.append(jnp.exp(z_pre[L - 1:L] - z_pre))

        att = {}
        for s, h in streams:
            q, k = load_q(s, h), load_k(s, h)
            acc = mlo_ref[n_lo] * _dot_nt(q.astype(BF16), k.astype(BF16))
            for t in range(n_lo):
                x = (jnp.where(upper[t], q, k) * e_all[s][t * L:(t + 1) * L, ck(h)]).astype(BF16)
                acc = acc + mlo_ref[t] * _dot_nt(x, x)
            rows8 = [acc[blk(b)] for b in range(nblk)]
            for t in range(n_lo, nlev):
                x = jnp.where(upper[t], q, k) * e_all[s][t * L:(t + 1) * L, ck(h)]
                ub = up_blocks[t - n_lo]
                lhs = jnp.concatenate([x[blk(b)] for b in ub], axis=0).astype(BF16)
                part = mup_ref[t - n_lo] * _dot_nt(lhs, x.astype(BF16))
                for n, b in enumerate(ub):
                    rows8[b] = rows8[b] + part[blk(n)]
            att[s, h] = jnp.concatenate(rows8, axis=0).astype(BF16)

        outs = {}
        for s, h in streams:
            e_pre = e_all[s][nlev * L:(nlev + 1) * L, ck(h)]
            e_suf = e_suf_all[s][:, ck(h)]
            v = v_ref[s, rs, cv(h)]
            st = st_ref[s, h]
            outs[s, h] = (_dot(att[s, h], v)
                          + _dot_nt((load_q(s, h) * e_pre).astype(BF16), st.astype(BF16)))
            st_ref[s, h] = st * e_pre[L - 1:L, :] + _dot_tn(v, (load_k(s, h) * e_suf).astype(BF16))

        for s, h in streams:
            o = outs[s, h]
            o = o * lax.rsqrt(jnp.mean(o * o, axis=-1, keepdims=True) + EPS) * g_ref[:, cv(h)]
            r = r_ref[s, rs, cv(h)].astype(F32)
            o_ref[s, rs, cv(h)] = (o * (r * _sigmoid(r))).astype(BF16)
        return carry

    lax.fori_loop(0, n_chunks, chunk, 0)

    @pl.when(t_idx == pl.num_programs(1) - 1)
    def _():
        sout_ref[...] = st_ref[...]


def _gla(qa, ka, va, ra, la, g_gla, s0t, *, n_seq, seq_len, tt, L, nb):
    nlev, n_lo, seg, msk_lo, msk_up, up_blocks = _gla_tables(L)
    assert n_seq % nb == 0 and seq_len % tt == 0 and tt % L == 0
    r3 = lambda x: x.reshape(n_seq, seq_len, x.shape[-1])
    row = lambda width: pl.BlockSpec((nb, tt, width), lambda b, t: (b, t, 0))
    st_spec = pl.BlockSpec((nb, H_A, DV_A, DK_A), lambda b, t: (b, 0, 0, 0))
    oa, st = pl.pallas_call(
        functools.partial(_gla_kernel, L=L, nlev=nlev, n_lo=n_lo, up_blocks=up_blocks,
                          n_chunks=tt // L, nb=nb),
        grid=(n_seq // nb, seq_len // tt),
        in_specs=[row(QK_A), row(QK_A), row(V_A), row(V_A), row(QK_A), _const_spec(seg.shape),
                  _const_spec(msk_lo.shape), _const_spec(msk_up.shape), _const_spec((1, V_A)), st_spec],
        out_specs=[row(V_A), st_spec],
        out_shape=[jax.ShapeDtypeStruct((n_seq, seq_len, V_A), BF16),
                   jax.ShapeDtypeStruct((n_seq, H_A, DV_A, DK_A), F32)],
        scratch_shapes=[pltpu.VMEM((nb, H_A, DV_A, DK_A), F32)],
        compiler_params=_params(("arbitrary", "arbitrary")),
        name="gla",
    )(r3(qa), r3(ka), r3(va), r3(ra), r3(la), jnp.asarray(seg, BF16), jnp.asarray(msk_lo),
      jnp.asarray(msk_up), g_gla, s0t)
    return oa.reshape(n_seq * seq_len, V_A), st


def _attn_kernel(*refs, widths, shifts):
    nb = len(widths)
    q_ref = refs[0]
    k_refs = refs[1:1 + nb]
    v_refs = refs[1 + nb:1 + 2 * nb]
    bias_ref = refs[1 + 2 * nb]
    o_ref = refs[2 + 2 * nb]
    t_idx = pl.program_id(1)
    tq = q_ref.shape[0]
    offs = np.concatenate([[0], np.cumsum(widths)]).tolist()
    same_width = len(set(widths)) == 1
    low = lax.broadcasted_iota(jnp.int32, (tq, LANES), 1) < HD_B
    low_t = lax.broadcasted_iota(jnp.int32, (LANES, tq), 0) < HD_B

    def lanes(h):
        return slice((h // 2) * LANES, (h // 2 + 1) * LANES)

    def scores(h):
        q2 = q_ref[:, lanes(h)].astype(F32) * (HD_B ** -0.5 * LOG2E)
        qm = jnp.where(low if h % 2 == 0 else ~low, q2, 0.0).astype(BF16)
        return [_dot_nt(k_refs[j][:, lanes(h)].astype(BF16), qm) for j in range(nb)]

    def key_reduce(xs, op, red):
        if same_width:
            return red(functools.reduce(op, xs), axis=0, keepdims=True)
        return functools.reduce(op, [red(x, axis=0, keepdims=True) for x in xs])

    def body(pens):
        def biased(raw, h):
            ss = []
            for j in range(nb):
                s = raw[j] + bias_ref[h, offs[j]:offs[j + 1], :]
                ss.append(s if pens[j] is None else s + pens[j])
            return ss, key_reduce(ss, jnp.maximum, jnp.max)

        raws = {0: scores(0), 1: scores(1), 2: scores(2)}
        staged = {0: biased(raws[0], 0), 1: biased(raws[1], 1)}
        prev = None
        for h in range(H_B):
            if h + 3 < H_B:
                raws[h + 3] = scores(h + 3)
            if h + 2 < H_B:
                staged[h + 2] = biased(raws.pop(h + 2), h + 2)
            ss, m = staged.pop(h)
            ps = [jnp.exp2(s - m) for s in ss]
            l = key_reduce(ps, jnp.add, jnp.sum)
            o_t = functools.reduce(jnp.add, [_dot_tn(v_refs[j][:, lanes(h)].astype(BF16), ps[j].astype(BF16))
                                             for j in range(nb)])
            o_t = o_t / l
            if h % 2 == 0:
                prev = o_t
            else:
                o_ref[:, lanes(h)] = jnp.where(low_t, prev, o_t).T.astype(BF16)

    n_early = max([-s for s in shifts if s is not None], default=0)
    if n_early == 0:
        body([None] * nb)
    else:
        @pl.when(t_idx >= n_early)
        def _():
            body([None] * nb)

        @pl.when(t_idx < n_early)
        def _():
            body([None if s is None else jnp.where(t_idx + s >= 0, 0.0, NEG).astype(F32) for s in shifts])


def _attn(q, kv_blocks, bias, *, n_seq, tq, tiles_per_seq, shifts):
    widths = tuple(b[2] for b in kv_blocks)
    k_specs = [pl.BlockSpec((b[2], W_B), b[3]) for b in kv_blocks]
    n = q.shape[0]
    return pl.pallas_call(
        functools.partial(_attn_kernel, widths=widths, shifts=tuple(shifts)),
        grid=(n_seq, tiles_per_seq),
        in_specs=[pl.BlockSpec((tq, W_B), lambda b, t: (b * tiles_per_seq + t, 0))]
                 + k_specs + k_specs + [_const_spec(bias.shape)],
        out_specs=pl.BlockSpec((tq, W_B), lambda b, t: (b * tiles_per_seq + t, 0)),
        out_shape=jax.ShapeDtypeStruct((n, W_B), BF16),
        compiler_params=_params(("arbitrary", "arbitrary")),
        name="band_attn",
    )(q, *[b[0] for b in kv_blocks], *[b[1] for b in kv_blocks], bias)


def _bias_kernel(row_ref, vis_ref, o_ref):
    nk, tq = vis_ref.shape
    period = row_ref.shape[-1]
    x = jnp.broadcast_to(row_ref[0], (nk, period))
    t = pltpu.roll(x, 0, 1, stride=1, stride_axis=0)
    o_ref[0] = jnp.where(vis_ref[...] > 0.0, t[:, :tq] * LOG2E, NEG)


def _band_bias(rel_bias, q_pos, k_pos, *, absolute):
    tq, nk = len(q_pos), len(k_pos)
    assert np.all(np.diff(q_pos) == 1) and np.all(np.diff(k_pos) == 1)
    period = -(-(nk + tq) // LANES) * LANES
    m = np.arange(period)
    m = np.where(m < tq, m, m - period)
    d = (q_pos[0] - k_pos[0]) + m
    row = rel_bias.astype(F32)[:, np.clip(d, -REL_CLIP, REL_CLIP) + REL_CLIP]
    qc = q_pos // CHUNK
    kc = k_pos // CHUNK
    vis = (kc[:, None] <= qc[None, :]) & (kc[:, None] >= qc[None, :] - BAND_CHUNKS)
    if absolute:
        vis = vis & (k_pos[:, None] >= 0)
    vis = vis.astype(np.float32)
    n_heads = row.shape[0]
    return pl.pallas_call(
        _bias_kernel,
        grid=(n_heads,),
        in_specs=[pl.BlockSpec((1, 1, period), lambda h: (h, 0, 0)), _const_spec(vis.shape)],
        out_specs=pl.BlockSpec((1, nk, tq), lambda h: (h, 0, 0)),
        out_shape=jax.ShapeDtypeStruct((n_heads, nk, tq), F32),
        compiler_params=_params(("arbitrary",)),
        name="band_bias",
    )(row.reshape(n_heads, 1, period), jnp.asarray(vis))


def _post_kernel(x_ref, oa_ref, ob_ref, ga_ref, gb_ref, pe_ref, cs0_ref,
                 wbra_ref, wbrb_ref, wout_ref, gpm_ref, gpf_ref, wup_ref, cw_ref, cb_ref,
                 wdn_ref, gqf_ref, gpp_ref, wpg_ref, wple_ref, gqp_ref,
                 y_ref, cso_ref, carry_ref, acc_ref, *, d_ff, fc, n_sub, seq_rows):
    t_idx = pl.program_id(1)
    ts = x_ref.shape[0] // n_sub
    n_fc = d_ff // fc
    packed = seq_rows is not None
    assert not packed or (n_sub == 1 and ts % seq_rows == 0 and seq_rows >= CONV_W - 1)
    n_pack = ts // seq_rows if packed else 1
    rows = lax.broadcasted_iota(jnp.int32, (ts, fc), 0)
    if packed:
        rows = rows % seq_rows
    st = [dict() for _ in range(n_sub)]
    rsl = lambda i: slice(i * ts, (i + 1) * ts)

    if not packed:
        @pl.when(t_idx == 0)
        def _():
            carry_ref[...] = cs0_ref[0]

    def history(j, cc):
        if not packed:
            return carry_ref[j:j + 1, cc]
        return jnp.concatenate([jnp.broadcast_to(cs0_ref[s, j:j + 1, cc], (seq_rows, fc))
                                for s in range(n_pack)], axis=0)

    def head1(i):
        r = rsl(i)
        st[i]["pw"] = _dot(pe_ref[r, :].astype(BF16), wple_ref[...])
        st[i]["mix"] = (_sigmoid(ga_ref[r, :].astype(F32)) * _dot(oa_ref[r, :], wbra_ref[...])
                        + _sigmoid(gb_ref[r, :].astype(F32)) * _dot(ob_ref[r, :], wbrb_ref[...])).astype(BF16)

    def head2(i):
        x1 = x_ref[rsl(i), :] + _rms(_dot(st[i].pop("mix"), wout_ref[...]), gpm_ref[...])
        st[i]["x1"] = x1
        st[i]["h2"] = _rms(x1, gpf_ref[...]).astype(BF16)

    def tail1(i):
        x2 = st[i].pop("x1") + _rms(acc_ref[i], gqf_ref[...])
        st[i]["x2"] = x2
        st[i]["hp"] = _rms(x2, gpp_ref[...]).astype(BF16)

    def tail2(i):
        gate = _sigmoid(_dot(st[i].pop("hp"), wpg_ref[...]))
        y_ref[rsl(i), :] = st[i].pop("x2") + _rms(gate * st[i].pop("pw"), gqp_ref[...])

    def ffn(i, hooks):
        h2 = st[i].pop("h2")

        def up(c):
            return (_dot(h2, wup_ref[:, c * fc:(c + 1) * fc]),
                    _dot(h2, wup_ref[:, d_ff + c * fc:d_ff + (c + 1) * fc]))

        nxt = up(0)
        for c in range(n_fc):
            cc = slice(c * fc, (c + 1) * fc)
            a, g = nxt
            if c + 1 < n_fc:
                nxt = up(c + 1)
            if c in hooks:
                hooks[c]()
            p2 = history(0, cc)
            p1 = history(1, cc)
            g1 = jnp.where(rows == 0, p1, pltpu.roll(g, 1, axis=0))
            g2 = jnp.where(rows == 0, p2, jnp.where(rows == 1, p1, pltpu.roll(g, 2, axis=0)))
            gc = cb_ref[:, cc] + cw_ref[0:1, cc] * g2 + cw_ref[1:2, cc] * g1 + cw_ref[2:3, cc] * g
            if packed:
                for s in range(n_pack):
                    cso_ref[s, :, cc] = g[(s + 1) * seq_rows - (CONV_W - 1):(s + 1) * seq_rows, :]
            else:
                carry_ref[:, cc] = g[ts - (CONV_W - 1):, :]
            u = (jax.nn.gelu(gc) * a).astype(BF16)
            d = _dot(u, wdn_ref[cc, :])
            if c == 0:
                acc_ref[i] = d
            else:
                acc_ref[i] += d

    head1(0)
    head2(0)
    for i in range(n_sub):
        hooks = {}
        if i + 1 < n_sub:
            hooks[1] = functools.partial(head1, i + 1)
            hooks[3] = functools.partial(head2, i + 1)
        if i >= 1:
            hooks[5] = functools.partial(tail1, i - 1)
            hooks[7] = functools.partial(tail2, i - 1)
        ffn(i, hooks)
    tail1(n_sub - 1)
    tail2(n_sub - 1)

    if not packed:
        @pl.when(t_idx == pl.num_programs(1) - 1)
        def _():
            cso_ref[0] = carry_ref[...]


def _post(x2d, oa, ob, ga, gb, pe2d, cs0, w, *, n_seq, seq_len, tm, fc, n_sub):
    n, d = x2d.shape
    d_ff = w["wdn"].shape[0]
    assert tm % n_sub == 0 and d_ff % fc == 0
    assert n_sub == 1 or d_ff // fc >= 8
    names = ["wbra", "wbrb", "wout", "gpm", "gpf", "wup", "cw", "cb", "wdn", "gqf", "gpp", "wpg", "wple", "gqp"]
    if tm == n and n_seq > 1:
        grid, seq_rows = (1, 1), seq_len
        row = lambda width: pl.BlockSpec((tm, width), lambda b, t: (0, 0))
        cs_spec = pl.BlockSpec((n_seq, CONV_W - 1, d_ff), lambda b, t: (0, 0, 0))
    else:
        assert seq_len % tm == 0
        tps = seq_len // tm
        grid, seq_rows = (n_seq, tps), None
        row = lambda width: pl.BlockSpec((tm, width), lambda b, t: (b * tps + t, 0))
        cs_spec = pl.BlockSpec((1, CONV_W - 1, d_ff), lambda b, t: (b, 0, 0))
    return pl.pallas_call(
        functools.partial(_post_kernel, d_ff=d_ff, fc=fc, n_sub=n_sub, seq_rows=seq_rows),
        grid=grid,
        in_specs=[row(d), row(V_A), row(W_B), row(d), row(d), row(pe2d.shape[1]), cs_spec]
                 + [_const_spec(w[k].shape) for k in names],
        out_specs=[row(d), cs_spec],
        out_shape=[jax.ShapeDtypeStruct((n, d), F32),
                   jax.ShapeDtypeStruct((n_seq, CONV_W - 1, d_ff), F32)],
        scratch_shapes=[pltpu.VMEM((CONV_W - 1, d_ff), F32), pltpu.VMEM((n_sub, tm // n_sub, d), F32)],
        compiler_params=_params(("arbitrary", "arbitrary")),
        name="post",
    )(x2d, oa, ob, ga, gb, pe2d, cs0, *[w[k] for k in names])


def _gla_seqs(b):
    return 2 if b % 2 == 0 else 1


def _prompt_tiles(b, t):
    post_rows = 256
    post_sub = 2 if t % (2 * post_rows) == 0 else 1
    return dict(gla_block=CHUNK, gla_seqs=_gla_seqs(b), tm_in=min(REACH, t), tq=min(256, t),
                tt=min(512, t), tm_post=min(post_rows * post_sub, t), post_sub=post_sub)


def _sample_tiles(b, t):
    return dict(gla_block=t, gla_seqs=_gla_seqs(b), tm_in=b * t, tq=t, tt=t, tm_post=b * t, post_sub=1)


def _layer(x, pe, s0, conv0, w, *, attn_cache, rel_bias, gla_block, gla_seqs, tm_in, tq, tt, tm_post,
           post_sub):
    b, t, d = x.shape
    n = b * t
    x2d = x.reshape(n, d)
    keep = min(REACH, t)
    if attn_cache is None:
        assert tm_in == keep and t % tm_in == 0
        tiles_per_seq = t // tm_in
    else:
        assert tm_in == n and keep == t
        tiles_per_seq = 1
    qa, ka, va, ra, la, qb, kb, vb, ga, gb, k_new, v_new = _in_proj(
        x2d, w["g_pre_mix"], w["w_in"], w["w_a2"], w["b_a2"], tm=tm_in, tiles_per_seq=tiles_per_seq)
    k_rows = k_new.reshape(b, keep, H_B, HD_B)
    v_rows = v_new.reshape(b, keep, H_B, HD_B)

    s0t = jnp.swapaxes(s0, -1, -2)
    oa, st = _gla(qa, ka, va, ra, la, w["g_gla"], s0t, n_seq=b, seq_len=t, tt=tt, L=gla_block,
                  nb=gla_seqs)
    s_new = jnp.swapaxes(st, -1, -2)

    if attn_cache is None:
        assert t % tq == 0 and tq % CHUNK == 0 and REACH % tq == 0
        tps = t // tq
        nback = REACH // tq
        q_pos = np.arange(tq)
        k_pos = np.arange(-REACH, tq)
        bias = _band_bias(rel_bias, q_pos, k_pos, absolute=False)
        blocks, shifts = [], []
        for j in range(nback + 1):
            sh = j - nback
            imap = functools.partial(lambda bb, tt_, sh_: (bb * tps + jnp.maximum(tt_ + sh_, 0), 0), sh_=sh)
            blocks.append((kb, vb, tq, imap))
            shifts.append(sh if sh < 0 else None)
        ob = _attn(qb, blocks, bias, n_seq=b, tq=tq, tiles_per_seq=tps, shifts=shifts)
    else:
        cache_k, cache_v = attn_cache
        lc = cache_k.shape[1]
        q_pos = PAST_LEN + np.arange(t)
        k_pos = np.concatenate([PAST_LEN - lc + np.arange(lc), PAST_LEN + np.arange(t)])
        bias = _band_bias(rel_bias, q_pos, k_pos, absolute=True)
        blocks = [(cache_k.reshape(b * lc, W_B), cache_v.reshape(b * lc, W_B), lc, lambda bb, tt_: (bb, 0)),
                  (kb, vb, t, lambda bb, tt_: (bb, 0))]
        ob = _attn(qb, blocks, bias, n_seq=b, tq=t, tiles_per_seq=1, shifts=[None, None])

    y, conv_new = _post(x2d, oa, ob, ga, gb, pe.reshape(n, pe.shape[-1]), conv0, w,
                        n_seq=b, seq_len=t, tm=tm_post, fc=256, n_sub=post_sub)
    return y.reshape(b, t, d), k_rows, v_rows, s_new, conv_new


def _prep_weights(g_pre_mix, w_in, w_a2, b_a2, g_gla, w_br_a, w_br_b, w_out, g_post_mix, g_pre_ffn,
                  w_up, conv_w, conv_b, w_down, g_post_ffn, g_pre_ple, w_ple_gate, w_ple, g_post_ple):
    d = w_in.shape[0]
    assert w_in.shape[1] == _A_END + GATE_RANK + 3 * W_B + 2 * d
    pad_rank = LANES - GATE_RANK
    w_parts = (w_in[:, :_A_END].astype(BF16),
               jnp.pad(w_in[:, _A_END:_A_END + GATE_RANK], ((0, 0), (0, pad_rank))).astype(BF16),
               w_in[:, _A_END + GATE_RANK:].astype(BF16))
    wa2_p = jnp.pad(w_a2, ((0, pad_rank), (0, 0))).astype(BF16)
    r1 = lambda v: v.reshape(1, -1).astype(F32)
    return dict(
        g_pre_mix=r1(g_pre_mix), w_in=w_parts, w_a2=wa2_p, b_a2=r1(b_a2), g_gla=r1(g_gla),
        wbra=w_br_a.astype(BF16), wbrb=w_br_b.astype(BF16), wout=w_out.astype(BF16),
        gpm=r1(g_post_mix), gpf=r1(g_pre_ffn), wup=w_up.astype(BF16), cw=conv_w.astype(F32),
        cb=r1(conv_b), wdn=w_down.astype(BF16), gqf=r1(g_post_ffn), gpp=r1(g_pre_ple),
        wpg=w_ple_gate.astype(BF16), wple=w_ple.astype(BF16), gqp=r1(g_post_ple))


def kernel(x_prompt, x_sample, cache_attn_k, cache_attn_v, state_gla, state_conv, p_prompt, p_sample,
           g_pre_mix, w_in, w_a2, b_a2, g_gla, rel_bias, w_br_a, w_br_b, w_out, g_post_mix, g_pre_ffn,
           w_up, conv_w, conv_b, w_down, g_post_ffn, g_pre_ple, w_ple_gate, w_ple, g_post_ple):
    depth = w_in.shape[0]
    bp, tp, _ = x_prompt.shape
    bs, ts, _ = x_sample.shape
    d_ff = w_down.shape[1]
    yp, ys = x_prompt, x_sample
    outs = [[] for _ in range(8)]
    for l in range(depth):
        w = _prep_weights(g_pre_mix[l], w_in[l], w_a2[l], b_a2[l], g_gla[l], w_br_a[l], w_br_b[l],
                          w_out[l], g_post_mix[l], g_pre_ffn[l], w_up[l], conv_w[l], conv_b[l],
                          w_down[l], g_post_ffn[l], g_pre_ple[l], w_ple_gate[l], w_ple[l], g_post_ple[l])
        yp, kp, vp, sp, cp = _layer(
            yp, p_prompt[l], jnp.zeros((bp, H_A, DK_A, DV_A), F32), jnp.zeros((bp, CONV_W - 1, d_ff), F32), w,
            attn_cache=None, rel_bias=rel_bias[l], **_prompt_tiles(bp, tp))
        ys, ks, vs, ss, cs = _layer(
            ys, p_sample[l], state_gla[l], state_conv[l], w,
            attn_cache=(cache_attn_k[l], cache_attn_v[l]), rel_bias=rel_bias[l], **_sample_tiles(bs, ts))
        for lst, val in zip(outs, (kp, vp, sp, cp, ks, vs, ss, cs)):
            lst.append(val)
    return (yp, ys) + tuple(jnp.stack(o) for o in outs)
```

```python
import functools

import numpy as np
import jax
import jax.numpy as jnp
from jax import lax
from jax.experimental import pallas as pl
from jax.experimental.pallas import tpu as pltpu

CHUNK = 64
H_A, DK_A, DV_A = 4, 128, 256
GATE_RANK = 16
GATE_TAU = 16.0
H_B, HD_B = 8, 64
BAND_CHUNKS = 8
REL_CLIP = 128
CONV_W = 3
PAST_LEN = 4096
EPS = 1e-6
REACH = BAND_CHUNKS * CHUNK
QK_A = H_A * DK_A
V_A = H_A * DV_A
W_B = H_B * HD_B

LANES = 128
SUBLANES = 8
VMEM_LIMIT_BYTES = 56 * 1024 * 1024
NEG = -1e30
LOG2E = 1.4426950408889634

F32 = jnp.float32
BF16 = jnp.bfloat16


def _dot(a, b):
    return jnp.dot(a, b, preferred_element_type=F32)


def _dot_nt(a, b):
    return lax.dot_general(a, b, (((1,), (1,)), ((), ())), preferred_element_type=F32)


def _dot_tn(a, b):
    return lax.dot_general(a, b, (((0,), (0,)), ((), ())), preferred_element_type=F32)


def _rms(x, g):
    return x * lax.rsqrt(jnp.mean(x * x, axis=-1, keepdims=True) + EPS) * g


def _sigmoid(x):
    return 1.0 / (1.0 + jnp.exp2(x * -LOG2E))


def _const_spec(shape):
    nd = len(shape)
    return pl.BlockSpec(shape, lambda *_: (0,) * nd, pipeline_mode=pl.Buffered(1))


def _params(sem):
    return pltpu.CompilerParams(dimension_semantics=sem, vmem_limit_bytes=VMEM_LIMIT_BYTES)


_A_QA, _A_KA, _A_VA, _A_RA, _A_END = 0, QK_A, 2 * QK_A, 2 * QK_A + V_A, 2 * QK_A + 2 * V_A
_B_QB, _B_KB, _B_VB, _B_GA = 0, W_B, 2 * W_B, 3 * W_B


def _in_proj_kernel(x_ref, g_ref, wa_ref, wal_ref, wb_ref, wa2_ref, ba2_ref,
                    qa_ref, ka_ref, va_ref, ra_ref, la_ref, qb_ref, kb_ref, vb_ref,
                    ga_ref, gb_ref, kt_ref, vt_ref, *, d_model):
    b_gb = _B_GA + d_model
    h = _rms(x_ref[...], g_ref[...]).astype(BF16)
    proj_a = lambda lo, hi: _dot(h, wa_ref[:, lo:hi])
    proj_b = lambda lo, hi: _dot(h, wb_ref[:, lo:hi])

    alr = _dot(h, wal_ref[...]).astype(BF16)
    qa_ref[...] = proj_a(_A_QA, _A_KA).astype(BF16)
    ka_ref[...] = proj_a(_A_KA, _A_VA).astype(BF16)
    logit = _dot(alr, wa2_ref[...]) + ba2_ref[...]
    va_ref[...] = proj_a(_A_VA, _A_RA).astype(BF16)
    ls = -(jnp.maximum(-logit, 0.0) + jnp.log(1.0 + jnp.exp(-jnp.abs(logit))))
    la_ref[...] = ls * (1.0 / GATE_TAU)
    ra_ref[...] = proj_a(_A_RA, _A_END).astype(BF16)
    qb_ref[...] = proj_b(_B_QB, _B_KB).astype(BF16)
    kb = proj_b(_B_KB, _B_VB)
    kb_ref[...] = kb.astype(BF16)
    kt_ref[0] = kb
    vb = proj_b(_B_VB, _B_GA)
    vb_ref[...] = vb.astype(BF16)
    vt_ref[0] = vb
    ga_ref[...] = proj_b(_B_GA, b_gb).astype(BF16)
    gb_ref[...] = proj_b(b_gb, b_gb + d_model).astype(BF16)


def _in_proj(x2d, g_pre, w_parts, wa2_p, ba2, *, tm, tiles_per_seq):
    n, d = x2d.shape
    n_tiles = n // tm
    n_seq = n_tiles // tiles_per_seq
    row = lambda width: pl.BlockSpec((tm, width), lambda i: (i, 0))
    tail = pl.BlockSpec((1, tm, W_B), lambda i: (i // tiles_per_seq, 0, 0))
    sd = jax.ShapeDtypeStruct
    outs = [
        (sd((n, QK_A), BF16), row(QK_A)),
        (sd((n, QK_A), BF16), row(QK_A)),
        (sd((n, V_A), BF16), row(V_A)),
        (sd((n, V_A), BF16), row(V_A)),
        (sd((n, QK_A), F32), row(QK_A)),
        (sd((n, W_B), BF16), row(W_B)),
        (sd((n, W_B), BF16), row(W_B)),
        (sd((n, W_B), BF16), row(W_B)),
        (sd((n, d), BF16), row(d)),
        (sd((n, d), BF16), row(d)),
        (sd((n_seq, tm, W_B), F32), tail),
        (sd((n_seq, tm, W_B), F32), tail),
    ]
    return pl.pallas_call(
        functools.partial(_in_proj_kernel, d_model=d),
        grid=(n_tiles,),
        in_specs=[row(d), _const_spec((1, d))] + [_const_spec(p.shape) for p in w_parts]
                 + [_const_spec(wa2_p.shape), _const_spec((1, QK_A))],
        out_specs=[o[1] for o in outs],
        out_shape=[o[0] for o in outs],
        compiler_params=_params(("arbitrary",)),
        name="in_proj",
    )(x2d, g_pre, *w_parts, wa2_p, ba2)


def _gla_tables(L):
    nlev = int(np.log2(L))
    assert 1 << nlev == L and L % SUBLANES == 0
    n_lo = min(int(np.log2(SUBLANES)), nlev)
    idx = np.arange(L)
    u = idx[None, :]
    i = idx[:, None]
    groups, masks = [], []
    for t in range(nlev):
        s = 1 << t
        start = (i >> t) << t
        upper = ((i >> t) & 1) == 1
        groups.append(np.where(upper, (u >= start) & (u <= i), (u > i) & (u <= start + s - 1)))
        masks.append(upper & (((u >> t) & 1) == 0) & ((i >> (t + 1)) == (u >> (t + 1))))
    groups.append(u <= i)
    seg = np.concatenate(groups, axis=0).astype(np.float32)
    seg2 = np.concatenate([seg, seg], axis=1)
    msk_lo = np.stack(masks[:n_lo] + [u == i], axis=0).astype(np.float32)
    up_blocks, msk_up = [], []
    for t in range(n_lo, nlev):
        ub = [b for b in range(L // SUBLANES) if ((b * SUBLANES) >> t) & 1]
        up_blocks.append(ub)
        msk_up.append(np.concatenate([masks[t][b * SUBLANES:(b + 1) * SUBLANES] for b in ub], axis=0))
    msk_up = np.stack(msk_up, axis=0).astype(np.float32) if msk_up else np.zeros((1, SUBLANES, L), np.float32)
    return nlev, n_lo, seg2, msk_lo, msk_up, up_blocks


def _gla_kernel(q_ref, k_ref, v_ref, r_ref, la_ref, seg_ref, mlo_ref, mup_ref, g_ref, s0_ref,
                o_ref, sout_ref, st_ref, *, L, nlev, n_lo, up_blocks, n_chunks, nb):
    t_idx = pl.program_id(1)
    nblk = L // SUBLANES
    blk = lambda b: slice(b * SUBLANES, (b + 1) * SUBLANES)

    @pl.when(t_idx == 0)
    def _():
        st_ref[...] = s0_ref[...]

    rows = lax.broadcasted_iota(jnp.int32, (L, DK_A), 0)
    upper = [((rows >> t) & 1) == 1 for t in range(nlev)]
    scale = DK_A ** -0.5
    nc = 2 if n_chunks % 2 == 0 else 1
    streams = [(c, s, h) for c in range(nc) for s in range(nb) for h in range(H_A)]

    def chunks(it, carry):
        rss = [pl.ds(pl.multiple_of((it * nc + c) * L, L), L) for c in range(nc)]
        ck = lambda h: slice(h * DK_A, (h + 1) * DK_A)
        cv = lambda h: slice(h * DV_A, (h + 1) * DV_A)
        load_q = lambda c, s, h: q_ref[s, rss[c], ck(h)].astype(F32)
        load_k = lambda c, s, h: k_ref[s, rss[c], ck(h)].astype(F32)

        e_all, e_suf_all = {}, {}
        for c in range(nc):
            for s in range(nb):
                a = la_ref[s, rss[c], :] * LOG2E
                a_hi = a.astype(BF16)
                a_lo = (a - a_hi.astype(F32)).astype(BF16)
                z = _dot(seg_ref[...], jnp.concatenate([a_hi, a_lo], axis=0))
                e_all[c, s] = jnp.exp2(z)
                z_pre = z[nlev * L:(nlev + 1) * L]
                e_suf_all[c, s] = jnp.exp2(z_pre[L - 1:L] - z_pre)

        att = {}
        for c, s, h in streams:
            q, k = load_q(c, s, h), load_k(c, s, h)
            acc = mlo_ref[n_lo] * _dot_nt(q.astype(BF16), k.astype(BF16))
            for t in range(n_lo):
                x = (jnp.where(upper[t], q, k) * e_all[c, s][t * L:(t + 1) * L, ck(h)]).astype(BF16)
                acc = acc + mlo_ref[t] * _dot_nt(x, x)
            rows8 = [acc[blk(b)] for b in range(nblk)]
            for t in range(n_lo, nlev):
                ub = up_blocks[t - n_lo]
                e_t = e_all[c, s][t * L:(t + 1) * L, ck(h)]
                xs = [(q if b in ub else k)[blk(b)] * e_t[blk(b)] for b in range(nblk)]
                lhs = jnp.concatenate([xs[b] for b in ub], axis=0).astype(BF16)
                part = mup_ref[t - n_lo] * _dot_nt(lhs, jnp.concatenate(xs, axis=0).astype(BF16))
                for n, b in enumerate(ub):
                    rows8[b] = rows8[b] + part[blk(n)]
            att[c, s, h] = jnp.concatenate(rows8, axis=0).astype(BF16)

        outs = {}
        for c, s, h in streams:
            e_pre = e_all[c, s][nlev * L:(nlev + 1) * L, ck(h)]
            e_suf = e_suf_all[c, s][:, ck(h)]
            v = v_ref[s, rss[c], cv(h)]
            st = st_ref[s, h]
            outs[c, s, h] = (_dot(att[c, s, h], v)
                             + _dot_nt((load_q(c, s, h) * e_pre).astype(BF16), st.astype(BF16)))
            st_ref[s, h] = st * e_pre[L - 1:L, :] + _dot_tn(v, (load_k(c, s, h) * e_suf).astype(BF16))

        for c, s, h in streams:
            o = outs[c, s, h]
            ms = jnp.mean(o * o, axis=-1, keepdims=True)
            o = o * (scale * lax.rsqrt(scale * scale * ms + EPS)) * g_ref[:, cv(h)]
            r = r_ref[s, rss[c], cv(h)].astype(F32)
            o_ref[s, rss[c], cv(h)] = (o * (r * _sigmoid(r))).astype(BF16)
        return carry

    lax.fori_loop(0, n_chunks // nc, chunks, 0)

    @pl.when(t_idx == pl.num_programs(1) - 1)
    def _():
        sout_ref[...] = st_ref[...]


def _gla(qa, ka, va, ra, la, g_gla, s0t, *, n_seq, seq_len, tt, L, nb):
    nlev, n_lo, seg, msk_lo, msk_up, up_blocks = _gla_tables(L)
    assert n_seq % nb == 0 and seq_len % tt == 0 and tt % L == 0
    r3 = lambda x: x.reshape(n_seq, seq_len, x.shape[-1])
    row = lambda width: pl.BlockSpec((nb, tt, width), lambda b, t: (b, t, 0))
    st_spec = pl.BlockSpec((nb, H_A, DV_A, DK_A), lambda b, t: (b, 0, 0, 0))
    oa, st = pl.pallas_call(
        functools.partial(_gla_kernel, L=L, nlev=nlev, n_lo=n_lo, up_blocks=up_blocks,
                          n_chunks=tt // L, nb=nb),
        grid=(n_seq // nb, seq_len // tt),
        in_specs=[row(QK_A), row(QK_A), row(V_A), row(V_A), row(QK_A), _const_spec(seg.shape),
                  _const_spec(msk_lo.shape), _const_spec(msk_up.shape), _const_spec((1, V_A)), st_spec],
        out_specs=[row(V_A), st_spec],
        out_shape=[jax.ShapeDtypeStruct((n_seq, seq_len, V_A), BF16),
                   jax.ShapeDtypeStruct((n_seq, H_A, DV_A, DK_A), F32)],
        scratch_shapes=[pltpu.VMEM((nb, H_A, DV_A, DK_A), F32)],
        compiler_params=_params(("arbitrary", "arbitrary")),
        name="gla",
    )(r3(qa), r3(ka), r3(va), r3(ra), r3(la), jnp.asarray(seg, BF16), jnp.asarray(msk_lo),
      jnp.asarray(msk_up), g_gla, s0t)
    return oa.reshape(n_seq * seq_len, V_A), st


def _attn_kernel(*refs, widths, shifts):
    nb = len(widths)
    q_ref = refs[0]
    k_refs = refs[1:1 + nb]
    v_refs = refs[1 + nb:1 + 2 * nb]
    bias_ref = refs[1 + 2 * nb]
    o_ref = refs[2 + 2 * nb]
    t_idx = pl.program_id(1)
    tq = q_ref.shape[0]
    offs = np.concatenate([[0], np.cumsum(widths)]).tolist()
    same_width = len(set(widths)) == 1
    low = lax.broadcasted_iota(jnp.int32, (tq, LANES), 1) < HD_B
    low_t = lax.broadcasted_iota(jnp.int32, (LANES, tq), 0) < HD_B

    def lanes(h):
        return slice((h // 2) * LANES, (h // 2 + 1) * LANES)

    def scores(h):
        q2 = q_ref[:, lanes(h)].astype(F32) * (HD_B ** -0.5 * LOG2E)
        qm = jnp.where(low if h % 2 == 0 else ~low, q2, 0.0).astype(BF16)
        return [_dot_nt(k_refs[j][:, lanes(h)].astype(BF16), qm) for j in range(nb)]

    def key_reduce(xs, op, red):
        if same_width:
            return red(functools.reduce(op, xs), axis=0, keepdims=True)
        return functools.reduce(op, [red(x, axis=0, keepdims=True) for x in xs])

    def body(pens):
        def biased(raw, h):
            ss = []
            for j in range(nb):
                s = raw[j] + bias_ref[h, offs[j]:offs[j + 1], :]
                ss.append(s if pens[j] is None else s + pens[j])
            return ss, key_reduce(ss, jnp.maximum, jnp.max)

        staged = {0: biased(scores(0), 0), 1: biased(scores(1), 1)}
        prev = None
        for h in range(H_B):
            if h + 2 < H_B:
                staged[h + 2] = biased(scores(h + 2), h + 2)
            ss, m = staged.pop(h)
            ps = [jnp.exp2(s - m) for s in ss]
            l = key_reduce(ps, jnp.add, jnp.sum)
            o_t = functools.reduce(jnp.add, [_dot_tn(v_refs[j][:, lanes(h)].astype(BF16), ps[j].astype(BF16))
                                             for j in range(nb)])
            o_t = o_t / l
            if h % 2 == 0:
                prev = o_t
            else:
                o_ref[:, lanes(h)] = jnp.where(low_t, prev, o_t).T.astype(BF16)

    n_early = max([-s for s in shifts if s is not None], default=0)
    if n_early == 0:
        body([None] * nb)
    else:
        @pl.when(t_idx >= n_early)
        def _():
            body([None] * nb)

        @pl.when(t_idx < n_early)
        def _():
            body([None if s is None else jnp.where(t_idx + s >= 0, 0.0, NEG).astype(F32) for s in shifts])


def _attn(q, kv_blocks, bias, *, n_seq, tq, tiles_per_seq, shifts):
    widths = tuple(b[2] for b in kv_blocks)
    k_specs = [pl.BlockSpec((b[2], W_B), b[3]) for b in kv_blocks]
    n = q.shape[0]
    return pl.pallas_call(
        functools.partial(_attn_kernel, widths=widths, shifts=tuple(shifts)),
        grid=(n_seq, tiles_per_seq),
        in_specs=[pl.BlockSpec((tq, W_B), lambda b, t: (b * tiles_per_seq + t, 0))]
                 + k_specs + k_specs + [_const_spec(bias.shape)],
        out_specs=pl.BlockSpec((tq, W_B), lambda b, t: (b * tiles_per_seq + t, 0)),
        out_shape=jax.ShapeDtypeStruct((n, W_B), BF16),
        compiler_params=_params(("arbitrary", "arbitrary")),
        name="band_attn",
    )(q, *[b[0] for b in kv_blocks], *[b[1] for b in kv_blocks], bias)


def _bias_kernel(row_ref, vis_ref, o_ref):
    nk, tq = vis_ref.shape
    period = row_ref.shape[-1]
    x = jnp.broadcast_to(row_ref[0], (nk, period))
    t = pltpu.roll(x, 0, 1, stride=1, stride_axis=0)
    o_ref[0] = jnp.where(vis_ref[...] > 0.0, t[:, :tq] * LOG2E, NEG)


def _band_bias(rel_bias, q_pos, k_pos, *, absolute):
    tq, nk = len(q_pos), len(k_pos)
    assert np.all(np.diff(q_pos) == 1) and np.all(np.diff(k_pos) == 1)
    period = -(-(nk + tq) // LANES) * LANES
    m = np.arange(period)
    m = np.where(m < tq, m, m - period)
    d = (q_pos[0] - k_pos[0]) + m
    row = rel_bias.astype(F32)[:, np.clip(d, -REL_CLIP, REL_CLIP) + REL_CLIP]
    qc = q_pos // CHUNK
    kc = k_pos // CHUNK
    vis = (kc[:, None] <= qc[None, :]) & (kc[:, None] >= qc[None, :] - BAND_CHUNKS)
    if absolute:
        vis = vis & (k_pos[:, None] >= 0)
    vis = vis.astype(np.float32)
    n_heads = row.shape[0]
    return pl.pallas_call(
        _bias_kernel,
        grid=(n_heads,),
        in_specs=[pl.BlockSpec((1, 1, period), lambda h: (h, 0, 0)), _const_spec(vis.shape)],
        out_specs=pl.BlockSpec((1, nk, tq), lambda h: (h, 0, 0)),
        out_shape=jax.ShapeDtypeStruct((n_heads, nk, tq), F32),
        compiler_params=_params(("arbitrary",)),
        name="band_bias",
    )(row.reshape(n_heads, 1, period), jnp.asarray(vis))


def _post_kernel(x_ref, oa_ref, ob_ref, ga_ref, gb_ref, pe_ref, cs0_ref,
                 wbra_ref, wbrb_ref, wout_ref, gpm_ref, gpf_ref, wup_ref, cw_ref, cb_ref,
                 wdn_ref, gqf_ref, gpp_ref, wpg_ref, wple_ref, gqp_ref,
                 y_ref, cso_ref, carry_ref, acc_ref, *, d_ff, fc, n_sub, seq_rows):
    t_idx = pl.program_id(1)
    ts = x_ref.shape[0] // n_sub
    n_fc = d_ff // fc
    packed = seq_rows is not None
    assert not packed or (n_sub == 1 and ts % seq_rows == 0 and seq_rows >= CONV_W - 1)
    n_pack = ts // seq_rows if packed else 1
    rows = lax.broadcasted_iota(jnp.int32, (ts, fc), 0)
    if packed:
        rows = rows % seq_rows
    st = [dict() for _ in range(n_sub)]
    rsl = lambda i: slice(i * ts, (i + 1) * ts)

    if not packed:
        @pl.when(t_idx == 0)
        def _():
            carry_ref[...] = cs0_ref[0]

    def history(j, cc):
        if not packed:
            return carry_ref[j:j + 1, cc]
        return jnp.concatenate([jnp.broadcast_to(cs0_ref[s, j:j + 1, cc], (seq_rows, fc))
                                for s in range(n_pack)], axis=0)

    def head1(i):
        r = rsl(i)
        st[i]["pw"] = _dot(pe_ref[r, :].astype(BF16), wple_ref[...])
        st[i]["mix"] = (_sigmoid(ga_ref[r, :].astype(F32)) * _dot(oa_ref[r, :], wbra_ref[...])
                        + _sigmoid(gb_ref[r, :].astype(F32)) * _dot(ob_ref[r, :], wbrb_ref[...])).astype(BF16)

    def head2(i):
        x1 = x_ref[rsl(i), :] + _rms(_dot(st[i].pop("mix"), wout_ref[...]), gpm_ref[...])
        st[i]["x1"] = x1
        st[i]["h2"] = _rms(x1, gpf_ref[...]).astype(BF16)

    def tail1(i):
        x2 = st[i].pop("x1") + _rms(acc_ref[i], gqf_ref[...])
        st[i]["x2"] = x2
        st[i]["hp"] = _rms(x2, gpp_ref[...]).astype(BF16)

    def tail2(i):
        gate = _sigmoid(_dot(st[i].pop("hp"), wpg_ref[...]))
        y_ref[rsl(i), :] = st[i].pop("x2") + _rms(gate * st[i].pop("pw"), gqp_ref[...])

    def ffn(i, hooks):
        h2 = st[i].pop("h2")

        def up(c):
            return (_dot(h2, wup_ref[:, c * fc:(c + 1) * fc]),
                    _dot(h2, wup_ref[:, d_ff + c * fc:d_ff + (c + 1) * fc]))

        nxt = up(0)
        for c in range(n_fc):
            cc = slice(c * fc, (c + 1) * fc)
            a, g = nxt
            if c + 1 < n_fc:
                nxt = up(c + 1)
            if c in hooks:
                hooks[c]()
            p2 = history(0, cc)
            p1 = history(1, cc)
            g1 = jnp.where(rows == 0, p1, pltpu.roll(g, 1, axis=0))
            g2 = jnp.where(rows == 0, p2, jnp.where(rows == 1, p1, pltpu.roll(g, 2, axis=0)))
            gc = cb_ref[:, cc] + cw_ref[0:1, cc] * g2 + cw_ref[1:2, cc] * g1 + cw_ref[2:3, cc] * g
            if packed:
                for s in range(n_pack):
                    cso_ref[s, :, cc] = g[(s + 1) * seq_rows - (CONV_W - 1):(s + 1) * seq_rows, :]
            else:
                carry_ref[:, cc] = g[ts - (CONV_W - 1):, :]
            u = (jax.nn.gelu(gc) * a).astype(BF16)
            d = _dot(u, wdn_ref[cc, :])
            if c == 0:
                acc_ref[i] = d
            else:
                acc_ref[i] += d

    head1(0)
    head2(0)
    for i in range(n_sub):
        hooks = {}
        if i + 1 < n_sub:
            hooks[1] = functools.partial(head1, i + 1)
            hooks[3] = functools.partial(head2, i + 1)
        if i >= 1:
            hooks[5] = functools.partial(tail1, i - 1)
            hooks[7] = functools.partial(tail2, i - 1)
        ffn(i, hooks)
    tail1(n_sub - 1)
    tail2(n_sub - 1)

    if not packed:
        @pl.when(t_idx == pl.num_programs(1) - 1)
        def _():
            cso_ref[0] = carry_ref[...]


def _post(x2d, oa, ob, ga, gb, pe2d, cs0, w, *, n_seq, seq_len, tm, fc, n_sub):
    n, d = x2d.shape
    d_ff = w["wdn"].shape[0]
    assert tm % n_sub == 0 and d_ff % fc == 0
    assert n_sub == 1 or d_ff // fc >= 8
    names = ["wbra", "wbrb", "wout", "gpm", "gpf", "wup", "cw", "cb", "wdn", "gqf", "gpp", "wpg", "wple", "gqp"]
    if tm == n and n_seq > 1:
        grid, seq_rows = (1, 1), seq_len
        row = lambda width: pl.BlockSpec((tm, width), lambda b, t: (0, 0))
        cs_spec = pl.BlockSpec((n_seq, CONV_W - 1, d_ff), lambda b, t: (0, 0, 0))
    else:
        assert seq_len % tm == 0
        tps = seq_len // tm
        grid, seq_rows = (n_seq, tps), None
        row = lambda width: pl.BlockSpec((tm, width), lambda b, t: (b * tps + t, 0))
        cs_spec = pl.BlockSpec((1, CONV_W - 1, d_ff), lambda b, t: (b, 0, 0))
    return pl.pallas_call(
        functools.partial(_post_kernel, d_ff=d_ff, fc=fc, n_sub=n_sub, seq_rows=seq_rows),
        grid=grid,
        in_specs=[row(d), row(V_A), row(W_B), row(d), row(d), row(pe2d.shape[1]), cs_spec]
                 + [_const_spec(w[k].shape) for k in names],
        out_specs=[row(d), cs_spec],
        out_shape=[jax.ShapeDtypeStruct((n, d), F32),
                   jax.ShapeDtypeStruct((n_seq, CONV_W - 1, d_ff), F32)],
        scratch_shapes=[pltpu.VMEM((CONV_W - 1, d_ff), F32), pltpu.VMEM((n_sub, tm // n_sub, d), F32)],
        compiler_params=_params(("arbitrary", "arbitrary")),
        name="post",
    )(x2d, oa, ob, ga, gb, pe2d, cs0, *[w[k] for k in names])


def _gla_seqs(b):
    return 2 if b % 2 == 0 else 1


def _prompt_tiles(b, t):
    post_rows = 256
    post_sub = 2 if t % (2 * post_rows) == 0 else 1
    return dict(gla_block=CHUNK, gla_seqs=_gla_seqs(b), tm_in=min(REACH, t), tq=min(256, t),
                tt=min(512, t), tm_post=min(post_rows * post_sub, t), post_sub=post_sub)


def _sample_tiles(b, t):
    return dict(gla_block=t, gla_seqs=_gla_seqs(b), tm_in=b * t, tq=t, tt=t, tm_post=b * t, post_sub=1)


def _layer(x, pe, s0, conv0, w, *, attn_cache, rel_bias, gla_block, gla_seqs, tm_in, tq, tt, tm_post,
           post_sub):
    b, t, d = x.shape
    n = b * t
    x2d = x.reshape(n, d)
    keep = min(REACH, t)
    if attn_cache is None:
        assert tm_in == keep and t % tm_in == 0
        tiles_per_seq = t // tm_in
    else:
        assert tm_in == n and keep == t
        tiles_per_seq = 1
    qa, ka, va, ra, la, qb, kb, vb, ga, gb, k_new, v_new = _in_proj(
        x2d, w["g_pre_mix"], w["w_in"], w["w_a2"], w["b_a2"], tm=tm_in, tiles_per_seq=tiles_per_seq)
    k_rows = k_new.reshape(b, keep, H_B, HD_B)
    v_rows = v_new.reshape(b, keep, H_B, HD_B)

    s0t = jnp.swapaxes(s0, -1, -2)
    oa, st = _gla(qa, ka, va, ra, la, w["g_gla"], s0t, n_seq=b, seq_len=t, tt=tt, L=gla_block,
                  nb=gla_seqs)
    s_new = jnp.swapaxes(st, -1, -2)

    if attn_cache is None:
        assert t % tq == 0 and tq % CHUNK == 0 and REACH % tq == 0
        tps = t // tq
        nback = REACH // tq
        q_pos = np.arange(tq)
        k_pos = np.arange(-REACH, tq)
        bias = _band_bias(rel_bias, q_pos, k_pos, absolute=False)
        blocks, shifts = [], []
        for j in range(nback + 1):
            sh = j - nback
            imap = functools.partial(lambda bb, tt_, sh_: (bb * tps + jnp.maximum(tt_ + sh_, 0), 0), sh_=sh)
            blocks.append((kb, vb, tq, imap))
            shifts.append(sh if sh < 0 else None)
        ob = _attn(qb, blocks, bias, n_seq=b, tq=tq, tiles_per_seq=tps, shifts=shifts)
    else:
        cache_k, cache_v = attn_cache
        lc = cache_k.shape[1]
        q_pos = PAST_LEN + np.arange(t)
        k_pos = np.concatenate([PAST_LEN - lc + np.arange(lc), PAST_LEN + np.arange(t)])
        bias = _band_bias(rel_bias, q_pos, k_pos, absolute=True)
        blocks = [(cache_k.reshape(b * lc, W_B), cache_v.reshape(b * lc, W_B), lc, lambda bb, tt_: (bb, 0)),
                  (kb, vb, t, lambda bb, tt_: (bb, 0))]
        ob = _attn(qb, blocks, bias, n_seq=b, tq=t, tiles_per_seq=1, shifts=[None, None])

    y, conv_new = _post(x2d, oa, ob, ga, gb, pe.reshape(n, pe.shape[-1]), conv0, w,
                        n_seq=b, seq_len=t, tm=tm_post, fc=256, n_sub=post_sub)
    return y.reshape(b, t, d), k_rows, v_rows, s_new, conv_new


def _prep_weights(g_pre_mix, w_in, w_a2, b_a2, g_gla, w_br_a, w_br_b, w_out, g_post_mix, g_pre_ffn,
                  w_up, conv_w, conv_b, w_down, g_post_ffn, g_pre_ple, w_ple_gate, w_ple, g_post_ple):
    d = w_in.shape[0]
    assert w_in.shape[1] == _A_END + GATE_RANK + 3 * W_B + 2 * d
    pad_rank = LANES - GATE_RANK
    w_parts = (w_in[:, :_A_END].astype(BF16),
               jnp.pad(w_in[:, _A_END:_A_END + GATE_RANK], ((0, 0), (0, pad_rank))).astype(BF16),
               w_in[:, _A_END + GATE_RANK:].astype(BF16))
    wa2_p = jnp.pad(w_a2, ((0, pad_rank), (0, 0))).astype(BF16)
    r1 = lambda v: v.reshape(1, -1).astype(F32)
    return dict(
        g_pre_mix=r1(g_pre_mix), w_in=w_parts, w_a2=wa2_p, b_a2=r1(b_a2), g_gla=r1(g_gla),
        wbra=w_br_a.astype(BF16), wbrb=w_br_b.astype(BF16), wout=w_out.astype(BF16),
        gpm=r1(g_post_mix), gpf=r1(g_pre_ffn), wup=w_up.astype(BF16), cw=conv_w.astype(F32),
        cb=r1(conv_b), wdn=w_down.astype(BF16), gqf=r1(g_post_ffn), gpp=r1(g_pre_ple),
        wpg=w_ple_gate.astype(BF16), wple=w_ple.astype(BF16), gqp=r1(g_post_ple))


def kernel(x_prompt, x_sample, cache_attn_k, cache_attn_v, state_gla, state_conv, p_prompt, p_sample,
           g_pre_mix, w_in, w_a2, b_a2, g_gla, rel_bias, w_br_a, w_br_b, w_out, g_post_mix, g_pre_ffn,
           w_up, conv_w, conv_b, w_down, g_post_ffn, g_pre_ple, w_ple_gate, w_ple, g_post_ple):
    depth = w_in.shape[0]
    bp, tp, _ = x_prompt.shape
    bs, ts, _ = x_sample.shape
    d_ff = w_down.shape[1]
    yp, ys = x_prompt, x_sample
    outs = [[] for _ in range(8)]
    for l in range(depth):
        w = _prep_weights(g_pre_mix[l], w_in[l], w_a2[l], b_a2[l], g_gla[l], w_br_a[l], w_br_b[l],
                          w_out[l], g_post_mix[l], g_pre_ffn[l], w_up[l], conv_w[l], conv_b[l],
                          w_down[l], g_post_ffn[l], g_pre_ple[l], w_ple_gate[l], w_ple[l], g_post_ple[l])
        yp, kp, vp, sp, cp = _layer(
            yp, p_prompt[l], jnp.zeros((bp, H_A, DK_A, DV_A), F32), jnp.zeros((bp, CONV_W - 1, d_ff), F32), w,
            attn_cache=None, rel_bias=rel_bias[l], **_prompt_tiles(bp, tp))
        ys, ks, vs, ss, cs = _layer(
            ys, p_sample[l], state_gla[l], state_conv[l], w,
            attn_cache=(cache_attn_k[l], cache_attn_v[l]), rel_bias=rel_bias[l], **_sample_tiles(bs, ts))
        for lst, val in zip(outs, (kp, vp, sp, cp, ks, vs, ss, cs)):
            lst.append(val)
    return (yp, ys) + tuple(jnp.stack(o) for o in outs)
```

```python
import functools

import numpy as np
import jax
import jax.numpy as jnp
from jax import lax
from jax.experimental import pallas as pl
from jax.experimental.pallas import tpu as pltpu

CHUNK = 64
H_A, DK_A, DV_A = 4, 128, 256
GATE_RANK = 16
GATE_TAU = 16.0
H_B, HD_B = 8, 64
BAND_CHUNKS = 8
REL_CLIP = 128
CONV_W = 3
PAST_LEN = 4096
EPS = 1e-6
REACH = BAND_CHUNKS * CHUNK
QK_A = H_A * DK_A
V_A = H_A * DV_A
W_B = H_B * HD_B

LANES = 128
SUBLANES = 8
VMEM_LIMIT_BYTES = 56 * 1024 * 1024
NEG = -1e30
LOG2E = 1.4426950408889634

F32 = jnp.float32
BF16 = jnp.bfloat16


def _dot(a, b):
    return jnp.dot(a, b, preferred_element_type=F32)


def _dot_nt(a, b):
    return lax.dot_general(a, b, (((1,), (1,)), ((), ())), preferred_element_type=F32)


def _dot_tn(a, b):
    return lax.dot_general(a, b, (((0,), (0,)), ((), ())), preferred_element_type=F32)


def _rms(x, g):
    return x * lax.rsqrt(jnp.mean(x * x, axis=-1, keepdims=True) + EPS) * g


def _sigmoid(x):
    return 1.0 / (1.0 + jnp.exp2(x * -LOG2E))


def _const_spec(shape):
    nd = len(shape)
    return pl.BlockSpec(shape, lambda *_: (0,) * nd, pipeline_mode=pl.Buffered(1))


def _params(sem):
    return pltpu.CompilerParams(dimension_semantics=sem, vmem_limit_bytes=VMEM_LIMIT_BYTES)


_A_QA, _A_KA, _A_VA, _A_RA, _A_END = 0, QK_A, 2 * QK_A, 2 * QK_A + V_A, 2 * QK_A + 2 * V_A
_B_QB, _B_KB, _B_VB, _B_GA = 0, W_B, 2 * W_B, 3 * W_B


def _in_proj_kernel(x_ref, g_ref, wa_ref, wal_ref, wb_ref, wa2_ref, ba2_ref,
                    qa_ref, ka_ref, va_ref, ra_ref, la_ref, qb_ref, kb_ref, vb_ref,
                    ga_ref, gb_ref, kt_ref, vt_ref, *, d_model):
    b_gb = _B_GA + d_model
    h = _rms(x_ref[...], g_ref[...]).astype(BF16)
    proj_a = lambda lo, hi: _dot(h, wa_ref[:, lo:hi])
    proj_b = lambda lo, hi: _dot(h, wb_ref[:, lo:hi])

    alr = _dot(h, wal_ref[...]).astype(BF16)
    qa_ref[...] = proj_a(_A_QA, _A_KA).astype(BF16)
    ka_ref[...] = proj_a(_A_KA, _A_VA).astype(BF16)
    logit = _dot(alr, wa2_ref[...]) + ba2_ref[...]
    va_ref[...] = proj_a(_A_VA, _A_RA).astype(BF16)
    ls = -(jnp.maximum(-logit, 0.0) + jnp.log(1.0 + jnp.exp(-jnp.abs(logit))))
    la_ref[...] = ls * (1.0 / GATE_TAU)
    ra_ref[...] = proj_a(_A_RA, _A_END).astype(BF16)
    qb_ref[...] = proj_b(_B_QB, _B_KB).astype(BF16)
    kb = proj_b(_B_KB, _B_VB)
    kb_ref[...] = kb.astype(BF16)
    kt_ref[0] = kb
    vb = proj_b(_B_VB, _B_GA)
    vb_ref[...] = vb.astype(BF16)
    vt_ref[0] = vb
    ga_ref[...] = proj_b(_B_GA, b_gb).astype(BF16)
    gb_ref[...] = proj_b(b_gb, b_gb + d_model).astype(BF16)


def _in_proj(x2d, g_pre, w_parts, wa2_p, ba2, *, tm, tiles_per_seq):
    n, d = x2d.shape
    n_tiles = n // tm
    n_seq = n_tiles // tiles_per_seq
    row = lambda width: pl.BlockSpec((tm, width), lambda i: (i, 0))
    tail = pl.BlockSpec((1, tm, W_B), lambda i: (i // tiles_per_seq, 0, 0))
    sd = jax.ShapeDtypeStruct
    outs = [
        (sd((n, QK_A), BF16), row(QK_A)),
        (sd((n, QK_A), BF16), row(QK_A)),
        (sd((n, V_A), BF16), row(V_A)),
        (sd((n, V_A), BF16), row(V_A)),
        (sd((n, QK_A), F32), row(QK_A)),
        (sd((n, W_B), BF16), row(W_B)),
        (sd((n, W_B), BF16), row(W_B)),
        (sd((n, W_B), BF16), row(W_B)),
        (sd((n, d), BF16), row(d)),
        (sd((n, d), BF16), row(d)),
        (sd((n_seq, tm, W_B), F32), tail),
        (sd((n_seq, tm, W_B), F32), tail),
    ]
    return pl.pallas_call(
        functools.partial(_in_proj_kernel, d_model=d),
        grid=(n_tiles,),
        in_specs=[row(d), _const_spec((1, d))] + [_const_spec(p.shape) for p in w_parts]
                 + [_const_spec(wa2_p.shape), _const_spec((1, QK_A))],
        out_specs=[o[1] for o in outs],
        out_shape=[o[0] for o in outs],
        compiler_params=_params(("arbitrary",)),
        name="in_proj",
    )(x2d, g_pre, *w_parts, wa2_p, ba2)


def _gla_tables(L):
    nlev = int(np.log2(L))
    assert 1 << nlev == L and L % SUBLANES == 0
    n_lo = min(int(np.log2(SUBLANES)), nlev)
    idx = np.arange(L)
    u = idx[None, :]
    i = idx[:, None]
    groups, masks = [], []
    for t in range(nlev):
        s = 1 << t
        start = (i >> t) << t
        upper = ((i >> t) & 1) == 1
        groups.append(np.where(upper, (u >= start) & (u <= i), (u > i) & (u <= start + s - 1)))
        masks.append(upper & (((u >> t) & 1) == 0) & ((i >> (t + 1)) == (u >> (t + 1))))
    groups.append(u <= i)
    seg = np.concatenate(groups, axis=0).astype(np.float32)
    seg2 = np.concatenate([seg, seg], axis=1)
    msk_lo = np.stack(masks[:n_lo] + [u == i], axis=0).astype(np.float32)
    up_blocks, msk_up = [], []
    for t in range(n_lo, nlev):
        ub = [b for b in range(L // SUBLANES) if ((b * SUBLANES) >> t) & 1]
        up_blocks.append(ub)
        msk_up.append(np.concatenate([masks[t][b * SUBLANES:(b + 1) * SUBLANES] for b in ub], axis=0))
    msk_up = np.stack(msk_up, axis=0).astype(np.float32) if msk_up else np.zeros((1, SUBLANES, L), np.float32)
    return nlev, n_lo, seg2, msk_lo, msk_up, up_blocks


def _gla_kernel(q_ref, k_ref, v_ref, r_ref, la_ref, seg_ref, mlo_ref, mup_ref, g_ref, s0_ref,
                o_ref, sout_ref, st_ref, *, L, nlev, n_lo, up_blocks, n_chunks, nb):
    t_idx = pl.program_id(1)
    nblk = L // SUBLANES
    blk = lambda b: slice(b * SUBLANES, (b + 1) * SUBLANES)

    @pl.when(t_idx == 0)
    def _():
        st_ref[...] = s0_ref[...]

    rows = lax.broadcasted_iota(jnp.int32, (L, DK_A), 0)
    upper = [((rows >> t) & 1) == 1 for t in range(nlev)]
    scale = DK_A ** -0.5
    nc = next(n for n in (4, 2, 1) if n_chunks % n == 0)
    streams = [(c, s, h) for c in range(nc) for s in range(nb) for h in range(H_A)]

    def chunks(it, carry):
        rss = [pl.ds(pl.multiple_of((it * nc + c) * L, L), L) for c in range(nc)]
        ck = lambda h: slice(h * DK_A, (h + 1) * DK_A)
        cv = lambda h: slice(h * DV_A, (h + 1) * DV_A)
        load_q = lambda c, s, h: q_ref[s, rss[c], ck(h)].astype(F32)
        load_k = lambda c, s, h: k_ref[s, rss[c], ck(h)].astype(F32)

        e_all, e_suf_all = {}, {}
        for c in range(nc):
            for s in range(nb):
                a = la_ref[s, rss[c], :] * LOG2E
                a_hi = a.astype(BF16)
                a_lo = (a - a_hi.astype(F32)).astype(BF16)
                z = _dot(seg_ref[...], jnp.concatenate([a_hi, a_lo], axis=0))
                e_all[c, s] = jnp.exp2(z)
                z_pre = z[nlev * L:(nlev + 1) * L]
                e_suf_all[c, s] = jnp.exp2(z_pre[L - 1:L] - z_pre)

        att = {}
        for c, s, h in streams:
            q, k = load_q(c, s, h), load_k(c, s, h)
            acc = mlo_ref[n_lo] * _dot_nt(q.astype(BF16), k.astype(BF16))
            for t in range(n_lo):
                x = (jnp.where(upper[t], q, k) * e_all[c, s][t * L:(t + 1) * L, ck(h)]).astype(BF16)
                acc = acc + mlo_ref[t] * _dot_nt(x, x)
            rows8 = [acc[blk(b)] for b in range(nblk)]
            for t in range(n_lo, nlev):
                ub = up_blocks[t - n_lo]
                e_t = e_all[c, s][t * L:(t + 1) * L, ck(h)]
                xs = [(q if b in ub else k)[blk(b)] * e_t[blk(b)] for b in range(nblk)]
                lhs = jnp.concatenate([xs[b] for b in ub], axis=0).astype(BF16)
                part = mup_ref[t - n_lo] * _dot_nt(lhs, jnp.concatenate(xs, axis=0).astype(BF16))
                for n, b in enumerate(ub):
                    rows8[b] = rows8[b] + part[blk(n)]
            att[c, s, h] = jnp.concatenate(rows8, axis=0).astype(BF16)

        outs = {}
        for c, s, h in streams:
            e_pre = e_all[c, s][nlev * L:(nlev + 1) * L, ck(h)]
            e_suf = e_suf_all[c, s][:, ck(h)]
            v = v_ref[s, rss[c], cv(h)]
            st = st_ref[s, h]
            outs[c, s, h] = (_dot(att[c, s, h], v)
                             + _dot_nt((load_q(c, s, h) * e_pre).astype(BF16), st.astype(BF16)))
            st_ref[s, h] = st * e_pre[L - 1:L, :] + _dot_tn(v, (load_k(c, s, h) * e_suf).astype(BF16))

        for c, s, h in streams:
            o = outs[c, s, h]
            ms = jnp.mean(o * o, axis=-1, keepdims=True)
            o = o * (scale * lax.rsqrt(scale * scale * ms + EPS)) * g_ref[:, cv(h)]
            r = r_ref[s, rss[c], cv(h)].astype(F32)
            o_ref[s, rss[c], cv(h)] = (o * (r * _sigmoid(r))).astype(BF16)
        return carry

    lax.fori_loop(0, n_chunks // nc, chunks, 0)

    @pl.when(t_idx == pl.num_programs(1) - 1)
    def _():
        sout_ref[...] = st_ref[...]


def _gla(qa, ka, va, ra, la, g_gla, s0t, *, n_seq, seq_len, tt, L, nb):
    nlev, n_lo, seg, msk_lo, msk_up, up_blocks = _gla_tables(L)
    assert n_seq % nb == 0 and seq_len % tt == 0 and tt % L == 0
    r3 = lambda x: x.reshape(n_seq, seq_len, x.shape[-1])
    row = lambda width: pl.BlockSpec((nb, tt, width), lambda b, t: (b, t, 0))
    st_spec = pl.BlockSpec((nb, H_A, DV_A, DK_A), lambda b, t: (b, 0, 0, 0))
    oa, st = pl.pallas_call(
        functools.partial(_gla_kernel, L=L, nlev=nlev, n_lo=n_lo, up_blocks=up_blocks,
                          n_chunks=tt // L, nb=nb),
        grid=(n_seq // nb, seq_len // tt),
        in_specs=[row(QK_A), row(QK_A), row(V_A), row(V_A), row(QK_A), _const_spec(seg.shape),
                  _const_spec(msk_lo.shape), _const_spec(msk_up.shape), _const_spec((1, V_A)), st_spec],
        out_specs=[row(V_A), st_spec],
        out_shape=[jax.ShapeDtypeStruct((n_seq, seq_len, V_A), BF16),
                   jax.ShapeDtypeStruct((n_seq, H_A, DV_A, DK_A), F32)],
        scratch_shapes=[pltpu.VMEM((nb, H_A, DV_A, DK_A), F32)],
        compiler_params=_params(("arbitrary", "arbitrary")),
        name="gla",
    )(r3(qa), r3(ka), r3(va), r3(ra), r3(la), jnp.asarray(seg, BF16), jnp.asarray(msk_lo),
      jnp.asarray(msk_up), g_gla, s0t)
    return oa.reshape(n_seq * seq_len, V_A), st


def _attn_kernel(*refs, widths, shifts):
    nb = len(widths)
    q_ref = refs[0]
    k_refs = refs[1:1 + nb]
    v_refs = refs[1 + nb:1 + 2 * nb]
    bias_ref = refs[1 + 2 * nb]
    o_ref = refs[2 + 2 * nb]
    t_idx = pl.program_id(1)
    tq = q_ref.shape[0]
    offs = np.concatenate([[0], np.cumsum(widths)]).tolist()
    same_width = len(set(widths)) == 1
    low = lax.broadcasted_iota(jnp.int32, (tq, LANES), 1) < HD_B
    low_t = lax.broadcasted_iota(jnp.int32, (LANES, tq), 0) < HD_B

    def lanes(h):
        return slice((h // 2) * LANES, (h // 2 + 1) * LANES)

    def scores(h):
        q2 = q_ref[:, lanes(h)].astype(F32) * (HD_B ** -0.5 * LOG2E)
        qm = jnp.where(low if h % 2 == 0 else ~low, q2, 0.0).astype(BF16)
        return [_dot_nt(k_refs[j][:, lanes(h)].astype(BF16), qm) for j in range(nb)]

    def key_reduce(xs, op, red):
        if same_width:
            return red(functools.reduce(op, xs), axis=0, keepdims=True)
        return functools.reduce(op, [red(x, axis=0, keepdims=True) for x in xs])

    def body(pens):
        def biased(raw, h):
            ss = []
            for j in range(nb):
                s = raw[j] + bias_ref[h, offs[j]:offs[j + 1], :]
                ss.append(s if pens[j] is None else s + pens[j])
            return ss, key_reduce(ss, jnp.maximum, jnp.max)

        staged = {0: biased(scores(0), 0), 1: biased(scores(1), 1), 2: biased(scores(2), 2)}
        prev = None
        for h in range(H_B):
            if h + 3 < H_B:
                staged[h + 3] = biased(scores(h + 3), h + 3)
            ss, m = staged.pop(h)
            ps = [jnp.exp2(s - m) for s in ss]
            l = key_reduce(ps, jnp.add, jnp.sum)
            o_t = functools.reduce(jnp.add, [_dot_tn(v_refs[j][:, lanes(h)].astype(BF16), ps[j].astype(BF16))
                                             for j in range(nb)])
            o_t = o_t / l
            if h % 2 == 0:
                prev = o_t
            else:
                o_ref[:, lanes(h)] = jnp.where(low_t, prev, o_t).T.astype(BF16)

    n_early = max([-s for s in shifts if s is not None], default=0)
    if n_early == 0:
        body([None] * nb)
    else:
        @pl.when(t_idx >= n_early)
        def _():
            body([None] * nb)

        @pl.when(t_idx < n_early)
        def _():
            body([None if s is None else jnp.where(t_idx + s >= 0, 0.0, NEG).astype(F32) for s in shifts])


def _attn(q, kv_blocks, bias, *, n_seq, tq, tiles_per_seq, shifts):
    widths = tuple(b[2] for b in kv_blocks)
    k_specs = [pl.BlockSpec((b[2], W_B), b[3]) for b in kv_blocks]
    n = q.shape[0]
    return pl.pallas_call(
        functools.partial(_attn_kernel, widths=widths, shifts=tuple(shifts)),
        grid=(n_seq, tiles_per_seq),
        in_specs=[pl.BlockSpec((tq, W_B), lambda b, t: (b * tiles_per_seq + t, 0))]
                 + k_specs + k_specs + [_const_spec(bias.shape)],
        out_specs=pl.BlockSpec((tq, W_B), lambda b, t: (b * tiles_per_seq + t, 0)),
        out_shape=jax.ShapeDtypeStruct((n, W_B), BF16),
        compiler_params=_params(("arbitrary", "arbitrary")),
        name="band_attn",
    )(q, *[b[0] for b in kv_blocks], *[b[1] for b in kv_blocks], bias)


def _bias_kernel(row_ref, vis_ref, o_ref):
    nk, tq = vis_ref.shape
    period = row_ref.shape[-1]
    x = jnp.broadcast_to(row_ref[0], (nk, period))
    t = pltpu.roll(x, 0, 1, stride=1, stride_axis=0)
    o_ref[0] = jnp.where(vis_ref[...] > 0.0, t[:, :tq] * LOG2E, NEG)


def _band_bias(rel_bias, q_pos, k_pos, *, absolute):
    tq, nk = len(q_pos), len(k_pos)
    assert np.all(np.diff(q_pos) == 1) and np.all(np.diff(k_pos) == 1)
    period = -(-(nk + tq) // LANES) * LANES
    m = np.arange(period)
    m = np.where(m < tq, m, m - period)
    d = (q_pos[0] - k_pos[0]) + m
    idx = np.clip(d, -REL_CLIP, REL_CLIP) + REL_CLIP
    table, pieces, lo = rel_bias.astype(F32), [], 0
    while lo < period:
        hi, step = lo + 1, 0
        if hi < period and idx[hi] - idx[lo] in (0, 1):
            step = idx[hi] - idx[lo]
            while hi < period and idx[hi] - idx[hi - 1] == step:
                hi += 1
        if step == 1:
            pieces.append(table[:, idx[lo]:idx[hi - 1] + 1])
        else:
            pieces.append(jnp.broadcast_to(table[:, idx[lo]:idx[lo] + 1], (table.shape[0], hi - lo)))
        lo = hi
    row = jnp.concatenate(pieces, axis=1)
    qc = q_pos // CHUNK
    kc = k_pos // CHUNK
    vis = (kc[:, None] <= qc[None, :]) & (kc[:, None] >= qc[None, :] - BAND_CHUNKS)
    if absolute:
        vis = vis & (k_pos[:, None] >= 0)
    vis = vis.astype(np.float32)
    n_heads = row.shape[0]
    return pl.pallas_call(
        _bias_kernel,
        grid=(n_heads,),
        in_specs=[pl.BlockSpec((1, 1, period), lambda h: (h, 0, 0)), _const_spec(vis.shape)],
        out_specs=pl.BlockSpec((1, nk, tq), lambda h: (h, 0, 0)),
        out_shape=jax.ShapeDtypeStruct((n_heads, nk, tq), F32),
        compiler_params=_params(("arbitrary",)),
        name="band_bias",
    )(row.reshape(n_heads, 1, period), jnp.asarray(vis))


def _post_kernel(x_ref, oa_ref, ob_ref, ga_ref, gb_ref, pe_ref, cs0_ref,
                 wbra_ref, wbrb_ref, wout_ref, gpm_ref, gpf_ref, wup_ref, cw_ref, cb_ref,
                 wdn_ref, gqf_ref, gpp_ref, wpg_ref, wple_ref, gqp_ref,
                 y_ref, cso_ref, carry_ref, acc_ref, *, d_ff, fc, n_sub, seq_rows):
    t_idx = pl.program_id(1)
    ts = x_ref.shape[0] // n_sub
    n_fc = d_ff // fc
    packed = seq_rows is not None
    assert not packed or (n_sub == 1 and ts % seq_rows == 0 and seq_rows >= CONV_W - 1)
    n_pack = ts // seq_rows if packed else 1
    rows = lax.broadcasted_iota(jnp.int32, (ts, fc), 0)
    if packed:
        rows = rows % seq_rows
    st = [dict() for _ in range(n_sub)]
    rsl = lambda i: slice(i * ts, (i + 1) * ts)

    if not packed:
        @pl.when(t_idx == 0)
        def _():
            carry_ref[...] = cs0_ref[0]

    def history(j, cc):
        if not packed:
            return carry_ref[j:j + 1, cc]
        return jnp.concatenate([jnp.broadcast_to(cs0_ref[s, j:j + 1, cc], (seq_rows, fc))
                                for s in range(n_pack)], axis=0)

    def head1(i):
        r = rsl(i)
        st[i]["pw"] = _dot(pe_ref[r, :].astype(BF16), wple_ref[...])
        st[i]["mix"] = (_sigmoid(ga_ref[r, :].astype(F32)) * _dot(oa_ref[r, :], wbra_ref[...])
                        + _sigmoid(gb_ref[r, :].astype(F32)) * _dot(ob_ref[r, :], wbrb_ref[...])).astype(BF16)

    def head2(i):
        x1 = x_ref[rsl(i), :] + _rms(_dot(st[i].pop("mix"), wout_ref[...]), gpm_ref[...])
        st[i]["x1"] = x1
        st[i]["h2"] = _rms(x1, gpf_ref[...]).astype(BF16)

    def tail1(i):
        x2 = st[i].pop("x1") + _rms(acc_ref[i], gqf_ref[...])
        st[i]["x2"] = x2
        st[i]["hp"] = _rms(x2, gpp_ref[...]).astype(BF16)

    def tail2(i):
        gate = _sigmoid(_dot(st[i].pop("hp"), wpg_ref[...]))
        y_ref[rsl(i), :] = st[i].pop("x2") + _rms(gate * st[i].pop("pw"), gqp_ref[...])

    def ffn(i, hooks):
        h2 = st[i].pop("h2")

        def up(c):
            return (_dot(h2, wup_ref[:, c * fc:(c + 1) * fc]),
                    _dot(h2, wup_ref[:, d_ff + c * fc:d_ff + (c + 1) * fc]))

        nxt = up(0)
        for c in range(n_fc):
            cc = slice(c * fc, (c + 1) * fc)
            a, g = nxt
            if c + 1 < n_fc:
                nxt = up(c + 1)
            if c in hooks:
                hooks[c]()
            p2 = history(0, cc)
            p1 = history(1, cc)
            g1 = jnp.where(rows == 0, p1, pltpu.roll(g, 1, axis=0))
            g2 = jnp.where(rows == 0, p2, jnp.where(rows == 1, p1, pltpu.roll(g, 2, axis=0)))
            gc = cb_ref[:, cc] + cw_ref[0:1, cc] * g2 + cw_ref[1:2, cc] * g1 + cw_ref[2:3, cc] * g
            if packed:
                for s in range(n_pack):
                    cso_ref[s, :, cc] = g[(s + 1) * seq_rows - (CONV_W - 1):(s + 1) * seq_rows, :]
            else:
                carry_ref[:, cc] = g[ts - (CONV_W - 1):, :]
            u = (jax.nn.gelu(gc) * a).astype(BF16)
            d = _dot(u, wdn_ref[cc, :])
            if c == 0:
                acc_ref[i] = d
            else:
                acc_ref[i] += d

    head1(0)
    head2(0)
    for i in range(n_sub):
        hooks = {}
        if i + 1 < n_sub:
            hooks[1] = functools.partial(head1, i + 1)
            hooks[3] = functools.partial(head2, i + 1)
        if i >= 1:
            hooks[5] = functools.partial(tail1, i - 1)
            hooks[7] = functools.partial(tail2, i - 1)
        ffn(i, hooks)
    tail1(n_sub - 1)
    tail2(n_sub - 1)

    if not packed:
        @pl.when(t_idx == pl.num_programs(1) - 1)
        def _():
            cso_ref[0] = carry_ref[...]


def _post(x2d, oa, ob, ga, gb, pe2d, cs0, w, *, n_seq, seq_len, tm, fc, n_sub):
    n, d = x2d.shape
    d_ff = w["wdn"].shape[0]
    assert tm % n_sub == 0 and d_ff % fc == 0
    assert n_sub == 1 or d_ff // fc >= 8
    names = ["wbra", "wbrb", "wout", "gpm", "gpf", "wup", "cw", "cb", "wdn", "gqf", "gpp", "wpg", "wple", "gqp"]
    if tm == n and n_seq > 1:
        grid, seq_rows = (1, 1), seq_len
        row = lambda width: pl.BlockSpec((tm, width), lambda b, t: (0, 0))
        cs_spec = pl.BlockSpec((n_seq, CONV_W - 1, d_ff), lambda b, t: (0, 0, 0))
    else:
        assert seq_len % tm == 0
        tps = seq_len // tm
        grid, seq_rows = (n_seq, tps), None
        row = lambda width: pl.BlockSpec((tm, width), lambda b, t: (b * tps + t, 0))
        cs_spec = pl.BlockSpec((1, CONV_W - 1, d_ff), lambda b, t: (b, 0, 0))
    return pl.pallas_call(
        functools.partial(_post_kernel, d_ff=d_ff, fc=fc, n_sub=n_sub, seq_rows=seq_rows),
        grid=grid,
        in_specs=[row(d), row(V_A), row(W_B), row(d), row(d), row(pe2d.shape[1]), cs_spec]
                 + [_const_spec(w[k].shape) for k in names],
        out_specs=[row(d), cs_spec],
        out_shape=[jax.ShapeDtypeStruct((n, d), F32),
                   jax.ShapeDtypeStruct((n_seq, CONV_W - 1, d_ff), F32)],
        scratch_shapes=[pltpu.VMEM((CONV_W - 1, d_ff), F32), pltpu.VMEM((n_sub, tm // n_sub, d), F32)],
        compiler_params=_params(("arbitrary", "arbitrary")),
        name="post",
    )(x2d, oa, ob, ga, gb, pe2d, cs0, *[w[k] for k in names])


def _gla_seqs(b):
    return 2 if b % 2 == 0 else 1


def _prompt_tiles(b, t):
    post_rows = 256
    post_sub = 2 if t % (2 * post_rows) == 0 else 1
    return dict(gla_block=CHUNK, gla_seqs=_gla_seqs(b), tm_in=min(REACH, t), tq=min(256, t),
                tt=min(512, t), tm_post=min(post_rows * post_sub, t), post_sub=post_sub)


def _sample_tiles(b, t):
    return dict(gla_block=t, gla_seqs=_gla_seqs(b), tm_in=b * t, tq=t, tt=t, tm_post=b * t, post_sub=1)


def _layer(x, pe, s0, conv0, w, *, attn_cache, rel_bias, gla_block, gla_seqs, tm_in, tq, tt, tm_post,
           post_sub):
    b, t, d = x.shape
    n = b * t
    x2d = x.reshape(n, d)
    keep = min(REACH, t)
    if attn_cache is None:
        assert tm_in == keep and t % tm_in == 0
        tiles_per_seq = t // tm_in
    else:
        assert tm_in == n and keep == t
        tiles_per_seq = 1
    qa, ka, va, ra, la, qb, kb, vb, ga, gb, k_new, v_new = _in_proj(
        x2d, w["g_pre_mix"], w["w_in"], w["w_a2"], w["b_a2"], tm=tm_in, tiles_per_seq=tiles_per_seq)
    k_rows = k_new.reshape(b, keep, H_B, HD_B)
    v_rows = v_new.reshape(b, keep, H_B, HD_B)

    s0t = jnp.swapaxes(s0, -1, -2)
    oa, st = _gla(qa, ka, va, ra, la, w["g_gla"], s0t, n_seq=b, seq_len=t, tt=tt, L=gla_block,
                  nb=gla_seqs)
    s_new = jnp.swapaxes(st, -1, -2)

    if attn_cache is None:
        assert t % tq == 0 and tq % CHUNK == 0 and REACH % tq == 0
        tps = t // tq
        nback = REACH // tq
        q_pos = np.arange(tq)
        k_pos = np.arange(-REACH, tq)
        bias = _band_bias(rel_bias, q_pos, k_pos, absolute=False)
        blocks, shifts = [], []
        for j in range(nback + 1):
            sh = j - nback
            imap = functools.partial(lambda bb, tt_, sh_: (bb * tps + jnp.maximum(tt_ + sh_, 0), 0), sh_=sh)
            blocks.append((kb, vb, tq, imap))
            shifts.append(sh if sh < 0 else None)
        ob = _attn(qb, blocks, bias, n_seq=b, tq=tq, tiles_per_seq=tps, shifts=shifts)
    else:
        cache_k, cache_v = attn_cache
        lc = cache_k.shape[1]
        q_pos = PAST_LEN + np.arange(t)
        k_pos = np.concatenate([PAST_LEN - lc + np.arange(lc), PAST_LEN + np.arange(t)])
        bias = _band_bias(rel_bias, q_pos, k_pos, absolute=True)
        blocks = [(cache_k.reshape(b * lc, W_B), cache_v.reshape(b * lc, W_B), lc, lambda bb, tt_: (bb, 0)),
                  (kb, vb, t, lambda bb, tt_: (bb, 0))]
        ob = _attn(qb, blocks, bias, n_seq=b, tq=t, tiles_per_seq=1, shifts=[None, None])

    y, conv_new = _post(x2d, oa, ob, ga, gb, pe.reshape(n, pe.shape[-1]), conv0, w,
                        n_seq=b, seq_len=t, tm=tm_post, fc=256, n_sub=post_sub)
    return y.reshape(b, t, d), k_rows, v_rows, s_new, conv_new


def _prep_weights(g_pre_mix, w_in, w_a2, b_a2, g_gla, w_br_a, w_br_b, w_out, g_post_mix, g_pre_ffn,
                  w_up, conv_w, conv_b, w_down, g_post_ffn, g_pre_ple, w_ple_gate, w_ple, g_post_ple):
    d = w_in.shape[0]
    assert w_in.shape[1] == _A_END + GATE_RANK + 3 * W_B + 2 * d
    pad_rank = LANES - GATE_RANK
    w_parts = (w_in[:, :_A_END].astype(BF16),
               jnp.pad(w_in[:, _A_END:_A_END + GATE_RANK], ((0, 0), (0, pad_rank))).astype(BF16),
               w_in[:, _A_END + GATE_RANK:].astype(BF16))
    wa2_p = jnp.pad(w_a2, ((0, pad_rank), (0, 0))).astype(BF16)
    r1 = lambda v: v.reshape(1, -1).astype(F32)
    return dict(
        g_pre_mix=r1(g_pre_mix), w_in=w_parts, w_a2=wa2_p, b_a2=r1(b_a2), g_gla=r1(g_gla),
        wbra=w_br_a.astype(BF16), wbrb=w_br_b.astype(BF16), wout=w_out.astype(BF16),
        gpm=r1(g_post_mix), gpf=r1(g_pre_ffn), wup=w_up.astype(BF16), cw=conv_w.astype(F32),
        cb=r1(conv_b), wdn=w_down.astype(BF16), gqf=r1(g_post_ffn), gpp=r1(g_pre_ple),
        wpg=w_ple_gate.astype(BF16), wple=w_ple.astype(BF16), gqp=r1(g_post_ple))


def kernel(x_prompt, x_sample, cache_attn_k, cache_attn_v, state_gla, state_conv, p_prompt, p_sample,
           g_pre_mix, w_in, w_a2, b_a2, g_gla, rel_bias, w_br_a, w_br_b, w_out, g_post_mix, g_pre_ffn,
           w_up, conv_w, conv_b, w_down, g_post_ffn, g_pre_ple, w_ple_gate, w_ple, g_post_ple):
    depth = w_in.shape[0]
    bp, tp, _ = x_prompt.shape
    bs, ts, _ = x_sample.shape
    d_ff = w_down.shape[1]
    yp, ys = x_prompt, x_sample
    outs = [[] for _ in range(8)]
    for l in range(depth):
        w = _prep_weights(g_pre_mix[l], w_in[l], w_a2[l], b_a2[l], g_gla[l], w_br_a[l], w_br_b[l],
                          w_out[l], g_post_mix[l], g_pre_ffn[l], w_up[l], conv_w[l], conv_b[l],
                          w_down[l], g_post_ffn[l], g_pre_ple[l], w_ple_gate[l], w_ple[l], g_post_ple[l])
        yp, kp, vp, sp, cp = _layer(
            yp, p_prompt[l], jnp.zeros((bp, H_A, DK_A, DV_A), F32), jnp.zeros((bp, CONV_W - 1, d_ff), F32), w,
            attn_cache=None, rel_bias=rel_bias[l], **_prompt_tiles(bp, tp))
        ys, ks, vs, ss, cs = _layer(
            ys, p_sample[l], state_gla[l], state_conv[l], w,
            attn_cache=(cache_attn_k[l], cache_attn_v[l]), rel_bias=rel_bias[l], **_sample_tiles(bs, ts))
        for lst, val in zip(outs, (kp, vp, sp, cp, ks, vs, ss, cs)):
            lst.append(val)
    return (yp, ys) + tuple(jnp.stack(o) for o in outs)
```
